```python
import math
import jax, jax.numpy as jnp
from jax import lax
import numpy as np

D_MODEL = 1024
BATCH = 2
SEQ = 8192
DEPTH = 1
DEC_BATCH = 128
DEC_SEQ = 4
PAST_LEN = 8192
PAGE_SIZE = 128

SWA_HEADS = 8
SWA_KV_HEADS = 2
SWA_HEAD_DIM = 64
SWA_GROUP = SWA_HEADS // SWA_KV_HEADS
SWA_WIDTH = SWA_HEADS * SWA_HEAD_DIM
KV_WIDTH = SWA_KV_HEADS * SWA_HEAD_DIM
WINDOW = 128
SWA_SCALE = SWA_HEAD_DIM ** -0.5
CHUNK = 128
CHUNK_GROUPS = 4
CHUNK_GROUP_DIM = 128
CHUNK_WIDTH = CHUNK_GROUPS * CHUNK_GROUP_DIM
MEM_LEN = 256
MEM_HEADS = 4
MEM_HEAD_DIM = 128
MEM_WIDTH = MEM_HEADS * MEM_HEAD_DIM
MEM_SCALE = MEM_HEAD_DIM ** -0.5
N_BRANCH = 3
BRANCH_WIDTH = 512
IN_WIDTH = SWA_WIDTH + 2 * KV_WIDTH + 2 * CHUNK_WIDTH + MEM_WIDTH + N_BRANCH * D_MODEL
PEER_HEADS = 8
PEER_N_KEYS = 128
PEER_N_EXPERTS = PEER_N_KEYS * PEER_N_KEYS
PEER_QUERY_DIM = 256
PEER_HALF = PEER_QUERY_DIM // 2
PEER_TOPK = 16
PEER_BLOCK = 128

EPS = 1e-6
NEG = -1e30

kernel_name = 'gated_chunkmlp_swa_mem_peer_step'


def rmsnorm(x, g):
    xf = x.astype(jnp.float32)
    xf = xf * lax.rsqrt(jnp.mean(xf * xf, axis=-1, keepdims=True) + EPS)
    return xf.astype(x.dtype) * g


def alibi_slopes():
    h = jnp.arange(1, SWA_HEADS + 1, dtype=jnp.float32)
    return jnp.exp2(-8.0 * h / SWA_HEADS).reshape(SWA_KV_HEADS, SWA_GROUP)


def sink_softmax(scores, sinks):
    m = jnp.maximum(jnp.max(scores, axis=-1, keepdims=True), sinks)
    e = jnp.exp(scores - m)
    return e / (jnp.sum(e, axis=-1, keepdims=True) + jnp.exp(sinks - m))


def in_proj(xn, p):
    B, T, _ = xn.shape
    proj = xn @ p['w_in']
    o1 = SWA_WIDTH
    o2 = o1 + KV_WIDTH
    o3 = o2 + KV_WIDTH
    o4 = o3 + 2 * CHUNK_WIDTH
    o5 = o4 + MEM_WIDTH
    q, k, v, uv, qm, gl = jnp.split(proj, [o1, o2, o3, o4, o5], axis=-1)
    u, vc = jnp.split(jax.nn.gelu(uv, approximate=False), 2, axis=-1)
    vc = rmsnorm(vc, p['chunk_norm_g'])
    return (q.reshape(B, T, SWA_KV_HEADS, SWA_GROUP, SWA_HEAD_DIM),
            k.reshape(B, T, SWA_KV_HEADS, SWA_HEAD_DIM),
            v.reshape(B, T, SWA_KV_HEADS, SWA_HEAD_DIM),
            u.reshape(B, T, CHUNK_GROUPS, CHUNK_GROUP_DIM),
            vc.reshape(B, T, CHUNK_GROUPS, CHUNK_GROUP_DIM),
            qm.reshape(B, T, MEM_HEADS, MEM_HEAD_DIM),
            gl.reshape(B, T, N_BRANCH, D_MODEL))


def swa_prompt(q, k, v, sinks):
    B, S = q.shape[:2]
    nb = S // WINDOW
    qb = q.reshape(B, nb, WINDOW, SWA_KV_HEADS, SWA_GROUP, SWA_HEAD_DIM)

    def band(t):
        tp = jnp.pad(t, ((0, 0), (WINDOW, 0), (0, 0), (0, 0)))
        tp = tp.reshape(B, nb + 1, WINDOW, SWA_KV_HEADS, SWA_HEAD_DIM)
        return jnp.concatenate([tp[:, :-1], tp[:, 1:]], axis=2)

    kb, vb = band(k), band(v)
    s = jnp.einsum('bnqkgd,bnskd->bnkgqs', qb, kb).astype(jnp.float32) * SWA_SCALE
    r = jnp.arange(WINDOW)
    c = jnp.arange(2 * WINDOW)
    dist = WINDOW + r[:, None] - c[None, :]
    key_pos = jnp.arange(nb)[:, None] * WINDOW - WINDOW + c[None, :]
    valid = ((dist >= 0) & (dist <= WINDOW))[None] & (key_pos >= 0)[:, None, :]
    s = s - alibi_slopes()[:, :, None, None] * dist.astype(jnp.float32)
    s = jnp.where(valid[None, :, None, None], s, NEG)
    w = sink_softmax(s, sinks.astype(jnp.float32).reshape(SWA_KV_HEADS, SWA_GROUP)[:, :, None, None])
    o = jnp.einsum('bnkgqs,bnskd->bnqkgd', w.astype(vb.dtype), vb)
    return o.reshape(B, S, SWA_WIDTH)


def swa_sample(q, k, v, ck, cv, sinks):
    Bd, T = q.shape[:2]
    kk = jnp.concatenate([ck, k], axis=1)
    vv = jnp.concatenate([cv, v], axis=1)
    s = jnp.einsum('btkgd,bskd->bkgts', q, kk).astype(jnp.float32) * SWA_SCALE
    dist = (WINDOW + jnp.arange(T))[:, None] - jnp.arange(WINDOW + T)[None, :]
    valid = (dist >= 0) & (dist <= WINDOW)
    s = s - alibi_slopes()[:, :, None, None] * dist.astype(jnp.float32)
    s = jnp.where(valid, s, NEG)
    w = sink_softmax(s, sinks.astype(jnp.float32).reshape(SWA_KV_HEADS, SWA_GROUP)[:, :, None, None])
    o = jnp.einsum('bkgts,bskd->btkgd', w.astype(vv.dtype), vv).reshape(Bd, T, SWA_WIDTH)
    return o, kk[:, T:], vv[:, T:]


def spatial_gate(u, v, ws, bs):
    L = u.shape[2]
    w = jnp.tril(ws[:, :L, :L])
    s = jnp.einsum('gts,bnsgc->bntgc', w, v) + bs[:, :L].T[None, None, :, :, None]
    return u * s


def mem_kv(mem, p):
    B, M, _ = mem.shape
    kv = rmsnorm(mem, p['mem_norm_g']) @ p['w_mem_kv']
    k, v = jnp.split(kv, 2, axis=-1)
    return (k.reshape(B, M, MEM_HEADS, MEM_HEAD_DIM), v.reshape(B, M, MEM_HEADS, MEM_HEAD_DIM))


def mem_attn(q, k, v):
    B, T = q.shape[:2]
    s = jnp.einsum('bthd,bmhd->bhtm', q, k).astype(jnp.float32) * MEM_SCALE
    w = jax.nn.softmax(s, axis=-1)
    return jnp.einsum('bhtm,bmhd->bthd', w.astype(v.dtype), v).reshape(B, T, MEM_WIDTH)


def peer(xn, p):
    B, T, D = xn.shape
    n = B * T
    nblk = -(-n // PEER_BLOCK)
    flat = jnp.pad(xn.reshape(n, D), ((0, nblk * PEER_BLOCK - n), (0, 0))).reshape(nblk, PEER_BLOCK, D)
    wq, sub, pu, pv = p['peer_wq'], p['peer_subkeys'], p['peer_u'], p['peer_v']

    def one(xb):
        t = xb.shape[0]
        q = (xb @ wq).reshape(t, PEER_HEADS, 2, PEER_HALF)
        s = jnp.einsum('thpd,pkd->thpk', q, sub).astype(jnp.float32)
        sv, si = lax.top_k(s, PEER_TOPK)
        cand = (sv[:, :, 0, :, None] + sv[:, :, 1, None, :]).reshape(t, PEER_HEADS, PEER_TOPK * PEER_TOPK)
        cv, ci = lax.top_k(cand, PEER_TOPK)
        e = (jnp.take_along_axis(si[:, :, 0], ci // PEER_TOPK, axis=-1) * PEER_N_KEYS
             + jnp.take_along_axis(si[:, :, 1], ci % PEER_TOPK, axis=-1))
        g = jax.nn.softmax(cv, axis=-1)
        a = jax.nn.gelu(jnp.einsum('td,thkd->thk', xb, pu[e]), approximate=False)
        return jnp.einsum('thk,thkd->td', g.astype(xb.dtype) * a, pv[e])

    out = lax.map(one, flat).reshape(nblk * PEER_BLOCK, D)[:n]
    return out.reshape(B, T, D)


def merge_and_channel(x, a_swa, a_chunk, a_mem, gl, p):
    br = jnp.stack([a_swa, a_chunk, a_mem], axis=2)
    proj = jnp.einsum('btjc,jcd->btjd', br, p['w_branch_out'])
    gate = jax.nn.sigmoid(gl + p['b_gate'])
    h = x + jnp.sum(gate * proj, axis=2) @ p['w_out']
    return h + peer(rmsnorm(h, p['norm2_g']), p)


def layer_prompt(x, mem, p):
    B, S, _ = x.shape
    q, k, v, u, vc, qm, gl = in_proj(rmsnorm(x, p['norm1_g']), p)
    a_swa = swa_prompt(q, k, v, p['swa_sinks'])
    nc = S // CHUNK
    a_chunk = spatial_gate(u.reshape(B, nc, CHUNK, CHUNK_GROUPS, CHUNK_GROUP_DIM),
                           vc.reshape(B, nc, CHUNK, CHUNK_GROUPS, CHUNK_GROUP_DIM),
                           p['chunk_ws'], p['chunk_bs']).reshape(B, S, CHUNK_WIDTH)
    mk, mv = mem_kv(mem, p)
    a_mem = mem_attn(qm, mk, mv)
    y = merge_and_channel(x, a_swa, a_chunk, a_mem, gl, p)
    return y, k[:, -WINDOW:], v[:, -WINDOW:], mk, mv


def layer_sample(x, ck, cv, cmk, cmv, p):
    Bd, T, _ = x.shape
    q, k, v, u, vc, qm, gl = in_proj(rmsnorm(x, p['norm1_g']), p)
    a_swa, nk, nv = swa_sample(q, k, v, ck, cv, p['swa_sinks'])
    a_chunk = spatial_gate(u[:, None], vc[:, None], p['chunk_ws'], p['chunk_bs'])[:, 0].reshape(Bd, T, CHUNK_WIDTH)
    a_mem = mem_attn(qm, cmk, cmv)
    y = merge_and_channel(x, a_swa, a_chunk, a_mem, gl, p)
    return y, nk, nv, vc


def setup_inputs(seed: int = 0) -> dict:
    key = jax.random.key(seed)
    ks = jax.random.split(key, 24)
    nrm = jax.random.normal
    f32 = jnp.float32
    return {
        'x_prompt': nrm(ks[0], (BATCH, SEQ, D_MODEL), f32),
        'x_sample': nrm(ks[1], (DEC_BATCH, DEC_SEQ, D_MODEL), f32),
        'cache_swa_k': nrm(ks[2], (DEPTH, DEC_BATCH, WINDOW, SWA_KV_HEADS, SWA_HEAD_DIM), f32),
        'cache_swa_v': nrm(ks[3], (DEPTH, DEC_BATCH, WINDOW, SWA_KV_HEADS, SWA_HEAD_DIM), f32),
        'cache_mem_k': nrm(ks[4], (DEPTH, DEC_BATCH, MEM_LEN, MEM_HEADS, MEM_HEAD_DIM), f32),
        'cache_mem_v': nrm(ks[5], (DEPTH, DEC_BATCH, MEM_LEN, MEM_HEADS, MEM_HEAD_DIM), f32),
        'mem_prompt': nrm(ks[6], (BATCH, MEM_LEN, D_MODEL), f32),
        'norm1_g': 1.0 + 0.05 * nrm(ks[7], (DEPTH, D_MODEL), f32),
        'w_in': nrm(ks[8], (DEPTH, D_MODEL, IN_WIDTH), f32) * D_MODEL ** -0.5,
        'b_gate': 0.01 * nrm(ks[9], (DEPTH, N_BRANCH, D_MODEL), f32),
        'swa_sinks': 0.5 * nrm(ks[10], (DEPTH, SWA_HEADS), f32),
        'chunk_norm_g': 1.0 + 0.05 * nrm(ks[11], (DEPTH, CHUNK_WIDTH), f32),
        'chunk_ws': nrm(ks[12], (DEPTH, CHUNK_GROUPS, CHUNK, CHUNK), f32) * CHUNK ** -0.5,
        'chunk_bs': 1.0 + 0.1 * nrm(ks[13], (DEPTH, CHUNK_GROUPS, CHUNK), f32),
        'mem_norm_g': 1.0 + 0.05 * nrm(ks[14], (DEPTH, D_MODEL), f32),
        'w_mem_kv': nrm(ks[15], (DEPTH, D_MODEL, 2 * MEM_WIDTH), f32) * D_MODEL ** -0.5,
        'w_branch_out': nrm(ks[16], (DEPTH, N_BRANCH, BRANCH_WIDTH, D_MODEL), f32) * BRANCH_WIDTH ** -0.5,
        'w_out': nrm(ks[17], (DEPTH, D_MODEL, D_MODEL), f32) * D_MODEL ** -0.5,
        'norm2_g': 1.0 + 0.05 * nrm(ks[18], (DEPTH, D_MODEL), f32),
        'peer_wq': nrm(ks[19], (DEPTH, D_MODEL, PEER_HEADS * PEER_QUERY_DIM), f32) * D_MODEL ** -0.5,
        'peer_subkeys': nrm(ks[20], (DEPTH, 2, PEER_N_KEYS, PEER_HALF), f32) * PEER_HALF ** -0.5,
        'peer_u': nrm(ks[21], (DEPTH, PEER_N_EXPERTS, D_MODEL), f32) * D_MODEL ** -0.5,
        'peer_v': nrm(ks[22], (DEPTH, PEER_N_EXPERTS, D_MODEL), f32) * D_MODEL ** -0.5,
        'final_norm_g': 1.0 + 0.05 * nrm(ks[23], (D_MODEL,), f32),
    }


def reference(x_prompt, x_sample, cache_swa_k, cache_swa_v, cache_mem_k, cache_mem_v, mem_prompt,
              norm1_g, w_in, b_gate, swa_sinks, chunk_norm_g, chunk_ws, chunk_bs, mem_norm_g, w_mem_kv,
              w_branch_out, w_out, norm2_g, peer_wq, peer_subkeys, peer_u, peer_v, final_norm_g):
    hp, hs = x_prompt, x_sample
    kp_l, vp_l, mkp_l, mvp_l, ks_l, vs_l, cv_l = [], [], [], [], [], [], []
    for l in range(DEPTH):
        p = {'norm1_g': norm1_g[l], 'w_in': w_in[l], 'b_gate': b_gate[l], 'swa_sinks': swa_sinks[l],
             'chunk_norm_g': chunk_norm_g[l], 'chunk_ws': chunk_ws[l], 'chunk_bs': chunk_bs[l],
             'mem_norm_g': mem_norm_g[l], 'w_mem_kv': w_mem_kv[l], 'w_branch_out': w_branch_out[l],
             'w_out': w_out[l], 'norm2_g': norm2_g[l], 'peer_wq': peer_wq[l],
             'peer_subkeys': peer_subkeys[l], 'peer_u': peer_u[l], 'peer_v': peer_v[l]}
        hp, kp, vp, mkp, mvp = layer_prompt(hp, mem_prompt, p)
        hs, ksn, vsn, cvs = layer_sample(hs, cache_swa_k[l], cache_swa_v[l], cache_mem_k[l], cache_mem_v[l], p)
        kp_l.append(kp)
        vp_l.append(vp)
        mkp_l.append(mkp)
        mvp_l.append(mvp)
        ks_l.append(ksn)
        vs_l.append(vsn)
        cv_l.append(cvs)
    y_prompt = rmsnorm(hp, final_norm_g)
    y_sample = rmsnorm(hs, final_norm_g)
    return (y_prompt, y_sample, jnp.stack(kp_l), jnp.stack(vp_l), jnp.stack(mkp_l), jnp.stack(mvp_l),
            jnp.stack(ks_l), jnp.stack(vs_l), jnp.stack(cv_l))
```

```python
import functools

import jax
import jax.numpy as jnp
from jax import lax
from jax.experimental import pallas as pl
from jax.experimental.pallas import tpu as pltpu

F32 = jnp.float32
BF16 = jnp.bfloat16

D_MODEL = 1024
SWA_HEADS = 8
SWA_KV_HEADS = 2
SWA_GROUP = SWA_HEADS // SWA_KV_HEADS
SWA_HEAD_DIM = 64
SWA_WIDTH = SWA_HEADS * SWA_HEAD_DIM
KV_WIDTH = SWA_KV_HEADS * SWA_HEAD_DIM
WINDOW = 128
SWA_SCALE = SWA_HEAD_DIM ** -0.5
CHUNK = 128
CHUNK_GROUPS = 4
CHUNK_GROUP_DIM = 128
CHUNK_WIDTH = CHUNK_GROUPS * CHUNK_GROUP_DIM
MEM_LEN = 256
MEM_HEADS = 4
MEM_HEAD_DIM = 128
MEM_WIDTH = MEM_HEADS * MEM_HEAD_DIM
MEM_SCALE = MEM_HEAD_DIM ** -0.5
N_BRANCH = 3
BRANCH_WIDTH = 512
PEER_HEADS = 8
PEER_N_KEYS = 128
PEER_N_EXPERTS = PEER_N_KEYS * PEER_N_KEYS
PEER_HALF = 128
PEER_TOPK = 16
EPS = 1e-6
NEG = -1e30

O_Q = 0
O_K = O_Q + SWA_WIDTH
O_V = O_K + KV_WIDTH
O_UV = O_V + KV_WIDTH
O_QM = O_UV + 2 * CHUNK_WIDTH
O_GL = O_QM + MEM_WIDTH
IN_WIDTH = O_GL + N_BRANCH * D_MODEL

TM_LAYER = 256
TP_PRE = 256
TM_DENSE = 512
EB_DENSE = 512
SB_SAMPLE = 8
VMEM_LIMIT = 56 * 1024 * 1024

_NT = (((1,), (1,)), ((), ()))


def _rms(x, g):
    return x * lax.rsqrt(jnp.mean(x * x, axis=-1, keepdims=True) + EPS) * g


def _gelu(x):
    return 0.5 * x * (1.0 + lax.erf(x * (2.0 ** -0.5)))


def _dot(a, b):
    return jnp.dot(a, b, preferred_element_type=F32)


def _dot_nt(a, b):
    return lax.dot_general(a, b, _NT, preferred_element_type=F32)


def _const_spec(shape):
    nd = len(shape)
    return pl.BlockSpec(shape, lambda *_: (0,) * nd, pipeline_mode=pl.Buffered(1))


def _smem_spec():
    return pl.BlockSpec(memory_space=pltpu.SMEM)


def _chunk_uv(xn, win_ref, cng_ref):
    uv = _gelu(_dot(xn, win_ref[:, O_UV:O_QM]))
    return uv[:, :CHUNK_WIDTH], _rms(uv[:, CHUNK_WIDTH:], cng_ref[...])


def _merge_tail(x, xn, branches, win_ref, bg_ref, wbo_ref, wout_ref, g2_ref, h_ref, xn2_ref):
    acc = None
    for j in range(N_BRANCH):
        pj = _dot(branches[j], wbo_ref[j])
        gl = _dot(xn, win_ref[:, O_GL + j * D_MODEL:O_GL + (j + 1) * D_MODEL]) + bg_ref[j:j + 1, :]
        term = jax.nn.sigmoid(gl) * pj
        acc = term if acc is None else acc + term
    h = x + _dot(acc.astype(BF16), wout_ref[...])
    h_ref[...] = h
    xn2_ref[...] = _rms(h, g2_ref[...]).astype(BF16)


def _softmax_rows(s):
    m = jnp.max(s, axis=-1, keepdims=True)
    e = jnp.exp(s - m)
    return e / jnp.sum(e, axis=-1, keepdims=True)


def _memkv_kernel(mem_ref, g_ref, w_ref, k_ref, v_ref, kb_ref, vb_ref):
    kv = _dot(_rms(mem_ref[...], g_ref[...]).astype(BF16), w_ref[...])
    k = kv[:, :MEM_WIDTH]
    v = kv[:, MEM_WIDTH:]
    k_ref[...] = k
    v_ref[...] = v
    kb_ref[...] = k.astype(BF16)
    vb_ref[...] = v.astype(BF16)


def _memkv_call(mem, g, w_b):
    b, m, d = mem.shape
    blk = lambda w: pl.BlockSpec((None, m, w), lambda i: (i, 0, 0))
    return pl.pallas_call(
        _memkv_kernel,
        grid=(b,),
        in_specs=[blk(d), _const_spec((1, d)), _const_spec((d, 2 * MEM_WIDTH))],
        out_specs=[blk(MEM_WIDTH)] * 4,
        out_shape=[jax.ShapeDtypeStruct((b, m, MEM_WIDTH), F32)] * 2
        + [jax.ShapeDtypeStruct((b, m, MEM_WIDTH), BF16)] * 2,
        compiler_params=pltpu.CompilerParams(dimension_semantics=("arbitrary",), vmem_limit_bytes=VMEM_LIMIT),
        name="mem_kv",
    )(mem, g, w_b)


def _prompt_layer_kernel(sinks_ref, x_ref, mk_ref, mv_ref, g1_ref, win_ref, cng_ref, ws_ref, bst_ref,
                         bg_ref, wbo_ref, wout_ref, g2_ref,
                         h_ref, xn2_ref, kout_ref, vout_ref,
                         kprev_ref, vprev_ref, aswa_ref, achk_ref, amem_ref, *, tiles_per_batch):
    tm = x_ref.shape[0]
    nblk = tm // WINDOW
    i = pl.program_id(0)
    first = (i % tiles_per_batch) == 0

    @pl.when(first)
    def _():
        kprev_ref[...] = jnp.zeros_like(kprev_ref)
        vprev_ref[...] = jnp.zeros_like(vprev_ref)

    x = x_ref[...]
    xn = _rms(x, g1_ref[...]).astype(BF16)

    q = _dot(xn, win_ref[:, O_Q:O_K]).astype(BF16)
    k = _dot(xn, win_ref[:, O_K:O_V])
    v = _dot(xn, win_ref[:, O_V:O_UV])
    kout_ref[...] = k[tm - WINDOW:]
    vout_ref[...] = v[tm - WINDOW:]
    kb = k.astype(BF16)
    vb = v.astype(BF16)
    r = lax.broadcasted_iota(jnp.int32, (WINDOW, 2 * WINDOW), 0)
    c = lax.broadcasted_iota(jnp.int32, (WINDOW, 2 * WINDOW), 1)
    dist = WINDOW + r - c
    band = (dist >= 0) & (dist <= WINDOW)
    distf = dist.astype(F32)
    first_lo = jnp.where(first, WINDOW, 0)
    for n in range(nblk):
        rows = slice(n * WINDOW, (n + 1) * WINDOW)
        if n == 0:
            kp, vp = kprev_ref[...], vprev_ref[...]
            valid = band & (c >= first_lo)
        else:
            prev = slice((n - 1) * WINDOW, n * WINDOW)
            kp, vp = kb[prev], vb[prev]
            valid = band
        kk = jnp.concatenate([kp, kb[rows]], axis=0)
        vv = jnp.concatenate([vp, vb[rows]], axis=0)
        for hq in range(SWA_HEADS):
            kvs = slice((hq // SWA_GROUP) * SWA_HEAD_DIM, (hq // SWA_GROUP + 1) * SWA_HEAD_DIM)
            hs = slice(hq * SWA_HEAD_DIM, (hq + 1) * SWA_HEAD_DIM)
            s = _dot_nt(q[rows, hs], kk[:, kvs]) * SWA_SCALE - (2.0 ** -(hq + 1)) * distf
            s = jnp.where(valid, s, NEG)
            sink = sinks_ref[hq]
            m = jnp.maximum(jnp.max(s, axis=-1, keepdims=True), sink)
            e = jnp.exp(s - m)
            w = e / (jnp.sum(e, axis=-1, keepdims=True) + jnp.exp(sink - m))
            aswa_ref[rows, hs] = _dot(w.astype(BF16), vv[:, kvs]).astype(BF16)
    kprev_ref[...] = kb[tm - WINDOW:]
    vprev_ref[...] = vb[tm - WINDOW:]

    u, vc = _chunk_uv(xn, win_ref, cng_ref)
    vcb = vc.astype(BF16)
    tri = (lax.broadcasted_iota(jnp.int32, (CHUNK, CHUNK), 0)
           >= lax.broadcasted_iota(jnp.int32, (CHUNK, CHUNK), 1))
    for g in range(CHUNK_GROUPS):
        gs = slice(g * CHUNK_GROUP_DIM, (g + 1) * CHUNK_GROUP_DIM)
        wg = jnp.where(tri, ws_ref[g], 0.0).astype(BF16)
        for n in range(nblk):
            rows = slice(n * CHUNK, (n + 1) * CHUNK)
            sg = _dot(wg, vcb[rows, gs]) + bst_ref[:, g:g + 1]
            achk_ref[rows, gs] = (u[rows, gs] * sg).astype(BF16)

    qm = _dot(xn, win_ref[:, O_QM:O_GL]).astype(BF16)
    for hm in range(MEM_HEADS):
        hs = slice(hm * MEM_HEAD_DIM, (hm + 1) * MEM_HEAD_DIM)
        w = _softmax_rows(_dot_nt(qm[:, hs], mk_ref[:, hs]) * MEM_SCALE)
        amem_ref[:, hs] = _dot(w.astype(BF16), mv_ref[:, hs]).astype(BF16)

    _merge_tail(x, xn, (aswa_ref[...], achk_ref[...], amem_ref[...]),
                win_ref, bg_ref, wbo_ref, wout_ref, g2_ref, h_ref, xn2_ref)


def _prompt_layer_call(x2, mkb, mvb, sinks, g1, w_in_b, cng, ws, bst, bg, wbo_b, wout_b, g2, *, batch, seq):
    n = x2.shape[0]
    tm = TM_LAYER
    tpb = seq // tm
    tile = lambda w: pl.BlockSpec((tm, w), lambda i: (i, 0))
    per_batch = lambda r, w: pl.BlockSpec((None, r, w), lambda i: (i // tpb, 0, 0))
    return pl.pallas_call(
        functools.partial(_prompt_layer_kernel, tiles_per_batch=tpb),
        grid=(n // tm,),
        in_specs=[_smem_spec(), tile(D_MODEL), per_batch(MEM_LEN, MEM_WIDTH), per_batch(MEM_LEN, MEM_WIDTH),
                  _const_spec((1, D_MODEL)), _const_spec((D_MODEL, IN_WIDTH)), _const_spec((1, CHUNK_WIDTH)),
                  _const_spec((CHUNK_GROUPS, CHUNK, CHUNK)), _const_spec((CHUNK, CHUNK_GROUPS)),
                  _const_spec((N_BRANCH, D_MODEL)), _const_spec((N_BRANCH, BRANCH_WIDTH, D_MODEL)),
                  _const_spec((D_MODEL, D_MODEL)), _const_spec((1, D_MODEL))],
        out_specs=[tile(D_MODEL), tile(D_MODEL), per_batch(WINDOW, KV_WIDTH), per_batch(WINDOW, KV_WIDTH)],
        out_shape=[jax.ShapeDtypeStruct((n, D_MODEL), F32), jax.ShapeDtypeStruct((n, D_MODEL), BF16),
                   jax.ShapeDtypeStruct((batch, WINDOW, KV_WIDTH), F32),
                   jax.ShapeDtypeStruct((batch, WINDOW, KV_WIDTH), F32)],
        scratch_shapes=[pltpu.VMEM((WINDOW, KV_WIDTH), BF16), pltpu.VMEM((WINDOW, KV_WIDTH), BF16),
                        pltpu.VMEM((tm, BRANCH_WIDTH), BF16), pltpu.VMEM((tm, BRANCH_WIDTH), BF16),
                        pltpu.VMEM((tm, BRANCH_WIDTH), BF16)],
        compiler_params=pltpu.CompilerParams(dimension_semantics=("arbitrary",), vmem_limit_bytes=VMEM_LIMIT),
        name="prompt_layer",
    )(sinks, x2, mkb, mvb, g1, w_in_b, cng, ws, bst, bg, wbo_b, wout_b, g2)


def _sample_proj_kernel(ws4_ref, bs4_ref, x_ref, g1_ref, win_ref, cng_ref,
                        qs_ref, qp0_ref, qp1_ref, qp2_ref, qp3_ref, k_ref, v_ref, vc_ref, achk_ref, qtmp_ref,
                        *, dec_seq):
    n = x_ref.shape[0]
    nseq = n // dec_seq
    xn = _rms(x_ref[...], g1_ref[...]).astype(BF16)
    q = _dot(xn, win_ref[:, O_Q:O_K])
    k_ref[...] = _dot(xn, win_ref[:, O_K:O_V])
    v_ref[...] = _dot(xn, win_ref[:, O_V:O_UV])
    qm = _dot(xn, win_ref[:, O_QM:O_GL])

    lane = lax.broadcasted_iota(jnp.int32, (n, 2 * SWA_HEAD_DIM), 1)
    for hq in range(SWA_HEADS):
        kvh = hq // SWA_GROUP
        win = q[:, (hq // 2) * 128:(hq // 2 + 1) * 128]
        if (hq % 2) != kvh:
            win = pltpu.roll(win, SWA_HEAD_DIM, 1)
        qtmp_ref[...] = jnp.where((lane // SWA_HEAD_DIM) == kvh, win, 0.0)
        for t in range(dec_seq):
            qs_ref[pl.ds(hq * dec_seq + t, nseq, stride=SWA_HEADS * dec_seq), :] = (
                qtmp_ref[pl.ds(t, nseq, stride=dec_seq), :])
    for hm, qp_ref in enumerate((qp0_ref, qp1_ref, qp2_ref, qp3_ref)):
        qp_ref[...] = jnp.zeros_like(qp_ref)
        qtmp_ref[...] = qm[:, hm * MEM_HEAD_DIM:(hm + 1) * MEM_HEAD_DIM]
        for t in range(dec_seq):
            qp_ref[pl.ds(hm * dec_seq + t, nseq, stride=MEM_HEADS * dec_seq), :] = (
                qtmp_ref[pl.ds(t, nseq, stride=dec_seq), :])

    u, vc = _chunk_uv(xn, win_ref, cng_ref)
    vc_ref[...] = vc
    tpos = lax.broadcasted_iota(jnp.int32, (n, CHUNK_GROUP_DIM), 0) % dec_seq
    for g in range(CHUNK_GROUPS):
        gs = slice(g * CHUNK_GROUP_DIM, (g + 1) * CHUNK_GROUP_DIM)
        vg = vc[:, gs]
        sg = jnp.zeros((n, CHUNK_GROUP_DIM), F32)
        for t in range(dec_seq):
            sg = jnp.where(tpos == t, bs4_ref[g * dec_seq + t], sg)
        for d in range(dec_seq):
            shifted = vg if d == 0 else pltpu.roll(vg, d, 0)
            coef = jnp.zeros((n, CHUNK_GROUP_DIM), F32)
            for t in range(d, dec_seq):
                coef = jnp.where(tpos == t, ws4_ref[(g * dec_seq + t) * dec_seq + (t - d)], coef)
            sg = sg + coef * shifted
        achk_ref[:, gs] = (u[:, gs] * sg).astype(BF16)


def _sample_proj_call(x2, ws4, bs4, g1, w_in_b, cng, *, dec_seq):
    n = x2.shape[0]
    nseq = n // dec_seq
    full = lambda r, w: pl.BlockSpec((r, w), lambda i: (0, 0))
    return pl.pallas_call(
        functools.partial(_sample_proj_kernel, dec_seq=dec_seq),
        grid=(1,),
        in_specs=[_smem_spec(), _smem_spec(), full(n, D_MODEL), _const_spec((1, D_MODEL)),
                  _const_spec((D_MODEL, IN_WIDTH)), _const_spec((1, CHUNK_WIDTH))],
        out_specs=[full(nseq * SWA_HEADS * dec_seq, 2 * SWA_HEAD_DIM)]
        + [full(nseq * MEM_HEADS * dec_seq, MEM_HEAD_DIM)] * MEM_HEADS
        + [full(n, KV_WIDTH), full(n, KV_WIDTH), full(n, CHUNK_WIDTH), full(n, CHUNK_WIDTH)],
        out_shape=[jax.ShapeDtypeStruct((nseq * SWA_HEADS * dec_seq, 2 * SWA_HEAD_DIM), F32)]
        + [jax.ShapeDtypeStruct((nseq * MEM_HEADS * dec_seq, MEM_HEAD_DIM), F32)] * MEM_HEADS
        + [jax.ShapeDtypeStruct((n, KV_WIDTH), F32), jax.ShapeDtypeStruct((n, KV_WIDTH), F32),
           jax.ShapeDtypeStruct((n, CHUNK_WIDTH), F32), jax.ShapeDtypeStruct((n, CHUNK_WIDTH), BF16)],
        scratch_shapes=[pltpu.VMEM((n, 128), F32)],
        compiler_params=pltpu.CompilerParams(dimension_semantics=("arbitrary",), vmem_limit_bytes=VMEM_LIMIT),
        name="sample_proj",
    )(ws4, bs4, x2, g1, w_in_b, cng)


def _sample_attn_kernel(sinks_ref, qs_ref, qp0_ref, qp1_ref, qp2_ref, qp3_ref, kn_ref, vn_ref,
                        ck_ref, cv_ref, cmk_ref, cmv_ref,
                        aswa_ref, amem_ref, nk_ref, nv_ref, kk_ref, vv_ref, *, dec_seq):
    sb = ck_ref.shape[0]
    rows_s = SWA_HEADS * dec_seq
    rows_m = MEM_HEADS * dec_seq
    kk_ref[WINDOW:, :] = jnp.zeros((WINDOW, KV_WIDTH), F32)
    vv_ref[WINDOW:, :] = jnp.zeros((WINDOW, KV_WIDTH), F32)

    m_idx = lax.broadcasted_iota(jnp.int32, (rows_s, 2 * WINDOW), 0)
    c_idx = lax.broadcasted_iota(jnp.int32, (rows_s, 2 * WINDOW), 1)
    dist = WINDOW + (m_idx % dec_seq) - c_idx
    valid = (dist >= 0) & (dist <= WINDOW)
    head = m_idx // dec_seq
    slope_dist = jnp.exp2(-(head + 1).astype(F32)) * dist.astype(F32)
    head_col = lax.broadcasted_iota(jnp.int32, (rows_s, 1), 0) // dec_seq
    sink = jnp.zeros((rows_s, 1), F32)
    for hq in range(SWA_HEADS):
        sink = jnp.where(head_col == hq, sinks_ref[hq], sink)
    lane_kv = lax.broadcasted_iota(jnp.int32, (2 * WINDOW, KV_WIDTH), 1) // SWA_HEAD_DIM
    lane_s = lax.broadcasted_iota(jnp.int32, (dec_seq, SWA_WIDTH), 1) // SWA_HEAD_DIM
    lane_m = lax.broadcasted_iota(jnp.int32, (dec_seq, MEM_WIDTH), 1) // MEM_HEAD_DIM

    for sl in range(sb):
        tok = slice(sl * dec_seq, (sl + 1) * dec_seq)
        kk_ref[0:WINDOW, :] = ck_ref[sl]
        vv_ref[0:WINDOW, :] = cv_ref[sl]
        kk_ref[WINDOW:WINDOW + dec_seq, :] = kn_ref[tok, :]
        vv_ref[WINDOW:WINDOW + dec_seq, :] = vn_ref[tok, :]
        nk_ref[sl] = kk_ref[dec_seq:dec_seq + WINDOW, :]
        nv_ref[sl] = vv_ref[dec_seq:dec_seq + WINDOW, :]

        qs = qs_ref[sl * rows_s:(sl + 1) * rows_s, :].astype(BF16)
        s = _dot_nt(qs, kk_ref[...].astype(BF16)) * SWA_SCALE - slope_dist
        s = jnp.where(valid, s, NEG)
        m = jnp.maximum(jnp.max(s, axis=-1, keepdims=True), sink)
        e = jnp.exp(s - m)
        w = e / (jnp.sum(e, axis=-1, keepdims=True) + jnp.exp(sink - m))
        vv = vv_ref[...]
        vr = pltpu.roll(vv, SWA_HEAD_DIM, 1)
        dup0 = jnp.where(lane_kv == 0, vv, vr).astype(BF16)
        dup1 = jnp.where(lane_kv == 1, vv, vr).astype(BF16)
        o = _dot(w.astype(BF16), jnp.concatenate([dup0, dup0, dup1, dup1], axis=1))
        acc = jnp.zeros((dec_seq, SWA_WIDTH), F32)
        for hq in range(SWA_HEADS):
            acc = acc + jnp.where(lane_s == hq, o[hq * dec_seq:(hq + 1) * dec_seq, :], 0.0)
        aswa_ref[tok, :] = acc

        qp = jnp.concatenate([r[sl * rows_m:(sl + 1) * rows_m, :] for r in (qp0_ref, qp1_ref, qp2_ref, qp3_ref)],
                             axis=1).astype(BF16)
        wm = _softmax_rows(_dot_nt(qp, cmk_ref[sl].astype(BF16)) * MEM_SCALE)
        om = _dot(wm.astype(BF16), cmv_ref[sl].astype(BF16))
        accm = jnp.zeros((dec_seq, MEM_WIDTH), F32)
        for hm in range(MEM_HEADS):
            accm = accm + jnp.where(lane_m == hm, om[hm * dec_seq:(hm + 1) * dec_seq, :], 0.0)
        amem_ref[tok, :] = accm


def _sample_attn_call(sinks, qs, qps, kn, vn, ck, cv, cmk, cmv, *, dec_seq):
    nseq = ck.shape[0]
    sb = SB_SAMPLE
    rows = lambda r, w: pl.BlockSpec((sb * r, w), lambda i: (i, 0))
    seqs = lambda r, w: pl.BlockSpec((sb, r, w), lambda i: (i, 0, 0))
    n = nseq * dec_seq
    return pl.pallas_call(
        functools.partial(_sample_attn_kernel, dec_seq=dec_seq),
        grid=(nseq // sb,),
        in_specs=[_smem_spec(), rows(SWA_HEADS * dec_seq, 2 * SWA_HEAD_DIM)]
        + [rows(MEM_HEADS * dec_seq, MEM_HEAD_DIM)] * MEM_HEADS + [
                  rows(dec_seq, KV_WIDTH), rows(dec_seq, KV_WIDTH),
                  seqs(WINDOW, KV_WIDTH), seqs(WINDOW, KV_WIDTH), seqs(MEM_LEN, MEM_WIDTH), seqs(MEM_LEN, MEM_WIDTH)],
        out_specs=[rows(dec_seq, SWA_WIDTH), rows(dec_seq, MEM_WIDTH), seqs(WINDOW, KV_WIDTH), seqs(WINDOW, KV_WIDTH)],
        out_shape=[jax.ShapeDtypeStruct((n, SWA_WIDTH), F32), jax.ShapeDtypeStruct((n, MEM_WIDTH), F32),
                   jax.ShapeDtypeStruct((nseq, WINDOW, KV_WIDTH), F32),
                   jax.ShapeDtypeStruct((nseq, WINDOW, KV_WIDTH), F32)],
        scratch_shapes=[pltpu.VMEM((2 * WINDOW, KV_WIDTH), F32), pltpu.VMEM((2 * WINDOW, KV_WIDTH), F32)],
        compiler_params=pltpu.CompilerParams(dimension_semantics=("arbitrary",), vmem_limit_bytes=VMEM_LIMIT),
        name="sample_attn",
    )(sinks, qs, *qps, kn, vn, ck, cv, cmk, cmv)


def _sample_merge_kernel(x_ref, aswa_ref, achk_ref, amem_ref, g1_ref, win_ref, bg_ref, wbo_ref, wout_ref, g2_ref,
                         h_ref, xn2_ref):
    x = x_ref[...]
    xn = _rms(x, g1_ref[...]).astype(BF16)
    _merge_tail(x, xn, (aswa_ref[...].astype(BF16), achk_ref[...], amem_ref[...].astype(BF16)),
                win_ref, bg_ref, wbo_ref, wout_ref, g2_ref, h_ref, xn2_ref)


def _sample_merge_call(x2, aswa, achk, amem, g1, w_in_b, bg, wbo_b, wout_b, g2):
    n = x2.shape[0]
    full = lambda w: pl.BlockSpec((n, w), lambda i: (0, 0))
    return pl.pallas_call(
        _sample_merge_kernel,
        grid=(1,),
        in_specs=[full(D_MODEL), full(BRANCH_WIDTH), full(BRANCH_WIDTH), full(BRANCH_WIDTH),
                  _const_spec((1, D_MODEL)), _const_spec((D_MODEL, IN_WIDTH)), _const_spec((N_BRANCH, D_MODEL)),
                  _const_spec((N_BRANCH, BRANCH_WIDTH, D_MODEL)), _const_spec((D_MODEL, D_MODEL)),
                  _const_spec((1, D_MODEL))],
        out_specs=[full(D_MODEL), full(D_MODEL)],
        out_shape=[jax.ShapeDtypeStruct((n, D_MODEL), F32), jax.ShapeDtypeStruct((n, D_MODEL), BF16)],
        compiler_params=pltpu.CompilerParams(dimension_semantics=("arbitrary",), vmem_limit_bytes=VMEM_LIMIT),
        name="sample_merge",
    )(x2, aswa, achk, amem, g1, w_in_b, bg, wbo_b, wout_b, g2)


def _top16(s, key_iota):
    rank = jnp.full(s.shape, float(PEER_TOPK), F32)
    vals = []
    for kth in range(PEER_TOPK):
        m = jnp.max(s, axis=0, keepdims=True)
        idx = jnp.min(jnp.where(s == m, key_iota, float(PEER_N_KEYS)), axis=0, keepdims=True)
        hit = key_iota == idx
        rank = jnp.where(hit, float(kth), rank)
        s = jnp.where(hit, -jnp.inf, s)
        vals.append(m)
    return rank, vals


def _peer_route_kernel(xn2_ref, wqt_ref, sub_ref, cnt_ref, rnk_ref, c1_ref, e2_ref):
    tp = xn2_ref.shape[0]
    xn2 = xn2_ref[...]
    lanes = 128
    key_iota = lax.broadcasted_iota(jnp.int32, (PEER_N_KEYS, lanes), 0).astype(F32)
    cand_iota = lax.broadcasted_iota(jnp.int32, (PEER_TOPK * PEER_TOPK, lanes), 0).astype(F32)
    k16 = lax.broadcasted_iota(jnp.int32, (PEER_TOPK, lanes), 0)

    def head_body(h, carry):
        st = []
        for p in range(2):
            row0 = pl.multiple_of((h * 2 + p) * PEER_HALF, PEER_HALF)
            qt = _dot_nt(wqt_ref[pl.ds(row0, PEER_HALF), :], xn2).astype(BF16)
            st.append(_dot(sub_ref[p], qt))
        for lg in range(tp // lanes):
            cols = slice(lg * lanes, (lg + 1) * lanes)
            s1 = st[0][:, cols]
            s2 = st[1][:, cols]
            rank1, v1 = _top16(s1, key_iota)
            rank2, v2 = _top16(s2, key_iota)
            top2 = jnp.zeros((PEER_TOPK, lanes), F32)
            for b in range(PEER_TOPK):
                top2 = jnp.where(k16 == b, v2[b], top2)
            cand = jnp.concatenate([v1[a] + top2 for a in range(PEER_TOPK)], axis=0)
            sel = jnp.zeros(cand.shape, F32)
            cmax = v1[0] + v2[0]
            z = jnp.zeros((1, lanes), F32)
            for _ in range(PEER_TOPK):
                m = jnp.max(cand, axis=0, keepdims=True)
                idx = jnp.min(jnp.where(cand == m, cand_iota, float(PEER_TOPK * PEER_TOPK)), axis=0, keepdims=True)
                hit = cand_iota == idx
                sel = jnp.where(hit, 1.0, sel)
                cand = jnp.where(hit, -jnp.inf, cand)
                z = z + jnp.exp(m - cmax)
            cnt = jnp.zeros((PEER_N_KEYS, lanes), F32)
            for a in range(PEER_TOPK):
                n_a = jnp.sum(sel[a * PEER_TOPK:(a + 1) * PEER_TOPK], axis=0, keepdims=True)
                cnt = jnp.where(rank1 == float(a), n_a, cnt)
            cnt_ref[h, :, cols] = cnt
            rnk_ref[h, :, cols] = rank2
            c1_ref[h, :, cols] = jnp.exp(s1 - v1[0])
            e2_ref[h, :, cols] = jnp.exp(s2 - v2[0]) / z
        return carry

    lax.fori_loop(0, PEER_HEADS, head_body, 0)


def _peer_route_call(xn2, wqt_b, sub_b):
    n = xn2.shape[0]
    tp = TP_PRE
    tab = pl.BlockSpec((PEER_HEADS, PEER_N_KEYS, tp), lambda i: (0, 0, i))
    return pl.pallas_call(
        _peer_route_kernel,
        grid=(n // tp,),
        in_specs=[pl.BlockSpec((tp, D_MODEL), lambda i: (i, 0)),
                  _const_spec((PEER_HEADS * 2 * PEER_HALF, D_MODEL)), _const_spec((2, PEER_N_KEYS, PEER_HALF))],
        out_specs=[tab] * 4,
        out_shape=[jax.ShapeDtypeStruct((PEER_HEADS, PEER_N_KEYS, n), F32)] * 4,
        compiler_params=pltpu.CompilerParams(dimension_semantics=("arbitrary",), vmem_limit_bytes=VMEM_LIMIT),
        name="peer_route",
    )(xn2, wqt_b, sub_b)


def _peer_dense_kernel(xn2_ref, pu_ref, pvt_ref, cnt_ref, rnk_ref, c1_ref, e2_ref, h_ref, gf_ref,
                       y_ref, acc_ref, g_ref):
    e = pl.program_id(1)
    eb = pu_ref.shape[0]
    keys_per_step = eb // PEER_N_KEYS

    @pl.when(e == 0)
    def _():
        acc_ref[...] = jnp.zeros_like(acc_ref)

    at = _dot_nt(pu_ref[...], xn2_ref[...])
    for ii in range(keys_per_step):
        i = e * keys_per_step + ii
        rows = slice(ii * PEER_N_KEYS, (ii + 1) * PEER_N_KEYS)
        w = None
        for h in range(PEER_HEADS):
            cnt_i = cnt_ref[h, pl.ds(i, 1), :]
            c1_i = c1_ref[h, pl.ds(i, 1), :]
            term = jnp.where(rnk_ref[h] < cnt_i, e2_ref[h], 0.0) * c1_i
            w = term if w is None else w + term
        g_ref[rows, :] = (_gelu(at[rows]) * w).astype(BF16)
    acc_ref[...] += _dot(pvt_ref[...], g_ref[...])

    @pl.when(e == pl.num_programs(1) - 1)
    def _():
        y_ref[...] = _rms(h_ref[...] + acc_ref[...].T, gf_ref[...])


def _peer_dense_call(xn2, pu_b, pvt_b, cnt, rnk, c1, e2, h, gf):
    n = xn2.shape[0]
    tm, eb = TM_DENSE, EB_DENSE
    tok = lambda dt: pl.BlockSpec((tm, D_MODEL), lambda t, e: (t, 0))
    tab = pl.BlockSpec((PEER_HEADS, PEER_N_KEYS, tm), lambda t, e: (0, 0, t))
    return pl.pallas_call(
        _peer_dense_kernel,
        grid=(n // tm, PEER_N_EXPERTS // eb),
        in_specs=[tok(BF16), pl.BlockSpec((eb, D_MODEL), lambda t, e: (e, 0)),
                  pl.BlockSpec((D_MODEL, eb), lambda t, e: (0, e)), tab, tab, tab, tab, tok(F32),
                  pl.BlockSpec((1, D_MODEL), lambda t, e: (0, 0))],
        out_specs=tok(F32),
        out_shape=jax.ShapeDtypeStruct((n, D_MODEL), F32),
        scratch_shapes=[pltpu.VMEM((D_MODEL, tm), F32), pltpu.VMEM((eb, tm), BF16)],
        compiler_params=pltpu.CompilerParams(dimension_semantics=("arbitrary", "arbitrary"),
                                             vmem_limit_bytes=VMEM_LIMIT),
        name="peer_dense",
    )(xn2, pu_b, pvt_b, cnt, rnk, c1, e2, h, gf)


def _peer_and_final_norm(h, xn2, wqt_b, sub_b, pu_b, pvt_b, gf):
    cnt, rnk, c1, e2 = _peer_route_call(xn2, wqt_b, sub_b)
    return _peer_dense_call(xn2, pu_b, pvt_b, cnt, rnk, c1, e2, h, gf)


def kernel(x_prompt, x_sample, cache_swa_k, cache_swa_v, cache_mem_k, cache_mem_v, mem_prompt,
           norm1_g, w_in, b_gate, swa_sinks, chunk_norm_g, chunk_ws, chunk_bs, mem_norm_g, w_mem_kv,
           w_branch_out, w_out, norm2_g, peer_wq, peer_subkeys, peer_u, peer_v, final_norm_g):
    depth = norm1_g.shape[0]
    assert depth == 1, "single-layer step"
    batch, seq, _ = x_prompt.shape
    nseq, dec_seq, _ = x_sample.shape

    g1 = norm1_g[0][None]
    g2 = norm2_g[0][None]
    gf = final_norm_g[None]
    cng = chunk_norm_g[0][None]
    w_in_b = w_in[0].astype(BF16)
    wbo_b = w_branch_out[0].astype(BF16)
    wout_b = w_out[0].astype(BF16)
    wqt_b = peer_wq[0].T.astype(BF16)
    sub_b = peer_subkeys[0].astype(BF16)
    pu_b = peer_u[0].astype(BF16)
    pvt_b = peer_v[0].T.astype(BF16)
    sinks = swa_sinks[0]
    bst = chunk_bs[0].T
    ws4 = chunk_ws[0][:, :dec_seq, :dec_seq].reshape(-1)
    bs4 = chunk_bs[0][:, :dec_seq].reshape(-1)

    mk, mv, mkb, mvb = _memkv_call(mem_prompt, mem_norm_g[0][None], w_mem_kv[0].astype(BF16))
    hp, xn2p, kp, vp = _prompt_layer_call(
        x_prompt.reshape(batch * seq, D_MODEL), mkb, mvb, sinks, g1, w_in_b, cng, chunk_ws[0], bst,
        b_gate[0], wbo_b, wout_b, g2, batch=batch, seq=seq)
    y_prompt = _peer_and_final_norm(hp, xn2p, wqt_b, sub_b, pu_b, pvt_b, gf).reshape(batch, seq, D_MODEL)

    xs = x_sample.reshape(nseq * dec_seq, D_MODEL)
    qs, *qps, kn, vn, vcs, achk = _sample_proj_call(xs, ws4, bs4, g1, w_in_b, cng, dec_seq=dec_seq)
    aswa, amem, nk, nv = _sample_attn_call(
        sinks, qs, qps, kn, vn,
        cache_swa_k[0].reshape(nseq, WINDOW, KV_WIDTH), cache_swa_v[0].reshape(nseq, WINDOW, KV_WIDTH),
        cache_mem_k[0].reshape(nseq, MEM_LEN, MEM_WIDTH), cache_mem_v[0].reshape(nseq, MEM_LEN, MEM_WIDTH),
        dec_seq=dec_seq)
    hs, xn2s = _sample_merge_call(xs, aswa, achk, amem, g1, w_in_b, b_gate[0], wbo_b, wout_b, g2)
    y_sample = _peer_and_final_norm(hs, xn2s, wqt_b, sub_b, pu_b, pvt_b, gf).reshape(nseq, dec_seq, D_MODEL)

    kv5 = lambda a, b: a.reshape(1, b, WINDOW, SWA_KV_HEADS, SWA_HEAD_DIM)
    mem5 = lambda a: a.reshape(1, batch, MEM_LEN, MEM_HEADS, MEM_HEAD_DIM)
    return (y_prompt, y_sample, kv5(kp, batch), kv5(vp, batch), mem5(mk), mem5(mv),
            kv5(nk, nseq), kv5(nv, nseq), vcs.reshape(1, nseq, dec_seq, CHUNK_GROUPS, CHUNK_GROUP_DIM))
```

```python
import functools

import jax
import jax.numpy as jnp
from jax import lax
from jax.experimental import pallas as pl
from jax.experimental.pallas import tpu as pltpu

F32 = jnp.float32
BF16 = jnp.bfloat16

D_MODEL = 1024
SWA_HEADS = 8
SWA_KV_HEADS = 2
SWA_GROUP = SWA_HEADS // SWA_KV_HEADS
SWA_HEAD_DIM = 64
SWA_WIDTH = SWA_HEADS * SWA_HEAD_DIM
KV_WIDTH = SWA_KV_HEADS * SWA_HEAD_DIM
WINDOW = 128
SWA_SCALE = SWA_HEAD_DIM ** -0.5
CHUNK = 128
CHUNK_GROUPS = 4
CHUNK_GROUP_DIM = 128
CHUNK_WIDTH = CHUNK_GROUPS * CHUNK_GROUP_DIM
MEM_LEN = 256
MEM_HEADS = 4
MEM_HEAD_DIM = 128
MEM_WIDTH = MEM_HEADS * MEM_HEAD_DIM
MEM_SCALE = MEM_HEAD_DIM ** -0.5
N_BRANCH = 3
BRANCH_WIDTH = 512
PEER_HEADS = 8
PEER_N_KEYS = 128
PEER_N_EXPERTS = PEER_N_KEYS * PEER_N_KEYS
PEER_HALF = 128
PEER_TOPK = 16
EPS = 1e-6
NEG = -1e30

O_Q = 0
O_K = O_Q + SWA_WIDTH
O_V = O_K + KV_WIDTH
O_UV = O_V + KV_WIDTH
O_QM = O_UV + 2 * CHUNK_WIDTH
O_GL = O_QM + MEM_WIDTH
IN_WIDTH = O_GL + N_BRANCH * D_MODEL

TM_LAYER = 256
TP_PRE = 256
TM_DENSE = 512
EB_DENSE = 512
SB_SAMPLE = 8
VMEM_LIMIT = 56 * 1024 * 1024

_NT = (((1,), (1,)), ((), ()))


def _rms(x, g):
    return x * lax.rsqrt(jnp.mean(x * x, axis=-1, keepdims=True) + EPS) * g


def _gelu(x):
    return 0.5 * x * (1.0 + lax.erf(x * (2.0 ** -0.5)))


def _dot(a, b):
    return jnp.dot(a, b, preferred_element_type=F32)


def _dot_nt(a, b):
    return lax.dot_general(a, b, _NT, preferred_element_type=F32)


def _const_spec(shape):
    nd = len(shape)
    return pl.BlockSpec(shape, lambda *_: (0,) * nd, pipeline_mode=pl.Buffered(1))


def _smem_spec():
    return pl.BlockSpec(memory_space=pltpu.SMEM)


def _chunk_uv(xn, win_ref, cng_ref):
    uv = _gelu(_dot(xn, win_ref[:, O_UV:O_QM]))
    return uv[:, :CHUNK_WIDTH], _rms(uv[:, CHUNK_WIDTH:], cng_ref[...])


def _merge_tail(x, xn, branches, win_ref, bg_ref, wbo_ref, wout_ref, g2_ref, h_ref, xn2_ref):
    acc = None
    for j in range(N_BRANCH):
        pj = _dot(branches[j], wbo_ref[j])
        gl = _dot(xn, win_ref[:, O_GL + j * D_MODEL:O_GL + (j + 1) * D_MODEL]) + bg_ref[j:j + 1, :]
        term = jax.nn.sigmoid(gl) * pj
        acc = term if acc is None else acc + term
    h = x + _dot(acc.astype(BF16), wout_ref[...])
    h_ref[...] = h
    xn2_ref[...] = _rms(h, g2_ref[...]).astype(BF16)


def _softmax_rows(s):
    m = jnp.max(s, axis=-1, keepdims=True)
    e = jnp.exp(s - m)
    return e / jnp.sum(e, axis=-1, keepdims=True)


def _memkv_kernel(mem_ref, g_ref, w_ref, k_ref, v_ref, kb_ref, vb_ref):
    kv = _dot(_rms(mem_ref[...], g_ref[...]).astype(BF16), w_ref[...])
    k = kv[:, :MEM_WIDTH]
    v = kv[:, MEM_WIDTH:]
    k_ref[...] = k
    v_ref[...] = v
    kb_ref[...] = k.astype(BF16)
    vb_ref[...] = v.astype(BF16)


def _memkv_call(mem, g, w_b):
    b, m, d = mem.shape
    blk = lambda w: pl.BlockSpec((None, m, w), lambda i: (i, 0, 0))
    return pl.pallas_call(
        _memkv_kernel,
        grid=(b,),
        in_specs=[blk(d), _const_spec((1, d)), _const_spec((d, 2 * MEM_WIDTH))],
        out_specs=[blk(MEM_WIDTH)] * 4,
        out_shape=[jax.ShapeDtypeStruct((b, m, MEM_WIDTH), F32)] * 2
        + [jax.ShapeDtypeStruct((b, m, MEM_WIDTH), BF16)] * 2,
        compiler_params=pltpu.CompilerParams(dimension_semantics=("arbitrary",), vmem_limit_bytes=VMEM_LIMIT),
        name="mem_kv",
    )(mem, g, w_b)


def _prompt_layer_kernel(sinks_ref, x_ref, mk_ref, mv_ref, g1_ref, win_ref, cng_ref, ws_ref, bst_ref,
                         bg_ref, wbo_ref, wout_ref, g2_ref,
                         h_ref, xn2_ref, kout_ref, vout_ref,
                         kprev_ref, vprev_ref, aswa_ref, achk_ref, amem_ref, *, tiles_per_batch):
    tm = x_ref.shape[0]
    nblk = tm // WINDOW
    i = pl.program_id(0)
    first = (i % tiles_per_batch) == 0

    @pl.when(first)
    def _():
        kprev_ref[...] = jnp.zeros_like(kprev_ref)
        vprev_ref[...] = jnp.zeros_like(vprev_ref)

    x = x_ref[...]
    xn = _rms(x, g1_ref[...]).astype(BF16)

    q = _dot(xn, win_ref[:, O_Q:O_K]).astype(BF16)
    k = _dot(xn, win_ref[:, O_K:O_V])
    v = _dot(xn, win_ref[:, O_V:O_UV])
    kout_ref[...] = k[tm - WINDOW:]
    vout_ref[...] = v[tm - WINDOW:]
    kb = k.astype(BF16)
    vb = v.astype(BF16)
    r = lax.broadcasted_iota(jnp.int32, (WINDOW, 2 * WINDOW), 0)
    c = lax.broadcasted_iota(jnp.int32, (WINDOW, 2 * WINDOW), 1)
    dist = WINDOW + r - c
    band = (dist >= 0) & (dist <= WINDOW)
    distf = dist.astype(F32)
    first_lo = jnp.where(first, WINDOW, 0)
    for n in range(nblk):
        rows = slice(n * WINDOW, (n + 1) * WINDOW)
        if n == 0:
            kp, vp = kprev_ref[...], vprev_ref[...]
            valid = band & (c >= first_lo)
        else:
            prev = slice((n - 1) * WINDOW, n * WINDOW)
            kp, vp = kb[prev], vb[prev]
            valid = band
        kk = jnp.concatenate([kp, kb[rows]], axis=0)
        vv = jnp.concatenate([vp, vb[rows]], axis=0)
        for hq in range(SWA_HEADS):
            kvs = slice((hq // SWA_GROUP) * SWA_HEAD_DIM, (hq // SWA_GROUP + 1) * SWA_HEAD_DIM)
            hs = slice(hq * SWA_HEAD_DIM, (hq + 1) * SWA_HEAD_DIM)
            s = _dot_nt(q[rows, hs], kk[:, kvs]) * SWA_SCALE - (2.0 ** -(hq + 1)) * distf
            s = jnp.where(valid, s, NEG)
            sink = sinks_ref[hq]
            m = jnp.maximum(jnp.max(s, axis=-1, keepdims=True), sink)
            e = jnp.exp(s - m)
            w = e / (jnp.sum(e, axis=-1, keepdims=True) + jnp.exp(sink - m))
            aswa_ref[rows, hs] = _dot(w.astype(BF16), vv[:, kvs]).astype(BF16)
    kprev_ref[...] = kb[tm - WINDOW:]
    vprev_ref[...] = vb[tm - WINDOW:]

    u, vc = _chunk_uv(xn, win_ref, cng_ref)
    vcb = vc.astype(BF16)
    tri = (lax.broadcasted_iota(jnp.int32, (CHUNK, CHUNK), 0)
           >= lax.broadcasted_iota(jnp.int32, (CHUNK, CHUNK), 1))
    for g in range(CHUNK_GROUPS):
        gs = slice(g * CHUNK_GROUP_DIM, (g + 1) * CHUNK_GROUP_DIM)
        wg = jnp.where(tri, ws_ref[g], 0.0).astype(BF16)
        for n in range(nblk):
            rows = slice(n * CHUNK, (n + 1) * CHUNK)
            sg = _dot(wg, vcb[rows, gs]) + bst_ref[:, g:g + 1]
            achk_ref[rows, gs] = (u[rows, gs] * sg).astype(BF16)

    qm = _dot(xn, win_ref[:, O_QM:O_GL]).astype(BF16)
    for hm in range(MEM_HEADS):
        hs = slice(hm * MEM_HEAD_DIM, (hm + 1) * MEM_HEAD_DIM)
        w = _softmax_rows(_dot_nt(qm[:, hs], mk_ref[:, hs]) * MEM_SCALE)
        amem_ref[:, hs] = _dot(w.astype(BF16), mv_ref[:, hs]).astype(BF16)

    _merge_tail(x, xn, (aswa_ref[...], achk_ref[...], amem_ref[...]),
                win_ref, bg_ref, wbo_ref, wout_ref, g2_ref, h_ref, xn2_ref)


def _prompt_layer_call(x2, mkb, mvb, sinks, g1, w_in_b, cng, ws, bst, bg, wbo_b, wout_b, g2, *, batch, seq):
    n = x2.shape[0]
    tm = TM_LAYER
    tpb = seq // tm
    tile = lambda w: pl.BlockSpec((tm, w), lambda i: (i, 0))
    per_batch = lambda r, w: pl.BlockSpec((None, r, w), lambda i: (i // tpb, 0, 0))
    return pl.pallas_call(
        functools.partial(_prompt_layer_kernel, tiles_per_batch=tpb),
        grid=(n // tm,),
        in_specs=[_smem_spec(), tile(D_MODEL), per_batch(MEM_LEN, MEM_WIDTH), per_batch(MEM_LEN, MEM_WIDTH),
                  _const_spec((1, D_MODEL)), _const_spec((D_MODEL, IN_WIDTH)), _const_spec((1, CHUNK_WIDTH)),
                  _const_spec((CHUNK_GROUPS, CHUNK, CHUNK)), _const_spec((CHUNK, CHUNK_GROUPS)),
                  _const_spec((N_BRANCH, D_MODEL)), _const_spec((N_BRANCH, BRANCH_WIDTH, D_MODEL)),
                  _const_spec((D_MODEL, D_MODEL)), _const_spec((1, D_MODEL))],
        out_specs=[tile(D_MODEL), tile(D_MODEL), per_batch(WINDOW, KV_WIDTH), per_batch(WINDOW, KV_WIDTH)],
        out_shape=[jax.ShapeDtypeStruct((n, D_MODEL), F32), jax.ShapeDtypeStruct((n, D_MODEL), BF16),
                   jax.ShapeDtypeStruct((batch, WINDOW, KV_WIDTH), F32),
                   jax.ShapeDtypeStruct((batch, WINDOW, KV_WIDTH), F32)],
        scratch_shapes=[pltpu.VMEM((WINDOW, KV_WIDTH), BF16), pltpu.VMEM((WINDOW, KV_WIDTH), BF16),
                        pltpu.VMEM((tm, BRANCH_WIDTH), BF16), pltpu.VMEM((tm, BRANCH_WIDTH), BF16),
                        pltpu.VMEM((tm, BRANCH_WIDTH), BF16)],
        compiler_params=pltpu.CompilerParams(dimension_semantics=("arbitrary",), vmem_limit_bytes=VMEM_LIMIT),
        name="prompt_layer",
    )(sinks, x2, mkb, mvb, g1, w_in_b, cng, ws, bst, bg, wbo_b, wout_b, g2)


def _sample_proj_kernel(ws4_ref, bs4_ref, x_ref, g1_ref, win_ref, cng_ref,
                        qs_ref, qp0_ref, qp1_ref, qp2_ref, qp3_ref, k_ref, v_ref, vc_ref, achk_ref, qtmp_ref,
                        *, dec_seq):
    n = x_ref.shape[0]
    nseq = n // dec_seq
    xn = _rms(x_ref[...], g1_ref[...]).astype(BF16)
    q = _dot(xn, win_ref[:, O_Q:O_K])
    k_ref[...] = _dot(xn, win_ref[:, O_K:O_V])
    v_ref[...] = _dot(xn, win_ref[:, O_V:O_UV])
    qm = _dot(xn, win_ref[:, O_QM:O_GL])

    lane = lax.broadcasted_iota(jnp.int32, (n, 2 * SWA_HEAD_DIM), 1)
    for hq in range(SWA_HEADS):
        kvh = hq // SWA_GROUP
        win = q[:, (hq // 2) * 128:(hq // 2 + 1) * 128]
        if (hq % 2) != kvh:
            win = pltpu.roll(win, SWA_HEAD_DIM, 1)
        qtmp_ref[...] = jnp.where((lane // SWA_HEAD_DIM) == kvh, win, 0.0)
        for t in range(dec_seq):
            qs_ref[pl.ds(hq * dec_seq + t, nseq, stride=SWA_HEADS * dec_seq), :] = (
                qtmp_ref[pl.ds(t, nseq, stride=dec_seq), :])
    for hm, qp_ref in enumerate((qp0_ref, qp1_ref, qp2_ref, qp3_ref)):
        qp_ref[...] = jnp.zeros_like(qp_ref)
        qtmp_ref[...] = qm[:, hm * MEM_HEAD_DIM:(hm + 1) * MEM_HEAD_DIM]
        for t in range(dec_seq):
            qp_ref[pl.ds(hm * dec_seq + t, nseq, stride=MEM_HEADS * dec_seq), :] = (
                qtmp_ref[pl.ds(t, nseq, stride=dec_seq), :])

    u, vc = _chunk_uv(xn, win_ref, cng_ref)
    vc_ref[...] = vc
    tpos = lax.broadcasted_iota(jnp.int32, (n, CHUNK_GROUP_DIM), 0) % dec_seq
    for g in range(CHUNK_GROUPS):
        gs = slice(g * CHUNK_GROUP_DIM, (g + 1) * CHUNK_GROUP_DIM)
        vg = vc[:, gs]
        sg = jnp.zeros((n, CHUNK_GROUP_DIM), F32)
        for t in range(dec_seq):
            sg = jnp.where(tpos == t, bs4_ref[g * dec_seq + t], sg)
        for d in range(dec_seq):
            shifted = vg if d == 0 else pltpu.roll(vg, d, 0)
            coef = jnp.zeros((n, CHUNK_GROUP_DIM), F32)
            for t in range(d, dec_seq):
                coef = jnp.where(tpos == t, ws4_ref[(g * dec_seq + t) * dec_seq + (t - d)], coef)
            sg = sg + coef * shifted
        achk_ref[:, gs] = (u[:, gs] * sg).astype(BF16)


def _sample_proj_call(x2, ws4, bs4, g1, w_in_b, cng, *, dec_seq):
    n = x2.shape[0]
    nseq = n // dec_seq
    full = lambda r, w: pl.BlockSpec((r, w), lambda i: (0, 0))
    return pl.pallas_call(
        functools.partial(_sample_proj_kernel, dec_seq=dec_seq),
        grid=(1,),
        in_specs=[_smem_spec(), _smem_spec(), full(n, D_MODEL), _const_spec((1, D_MODEL)),
                  _const_spec((D_MODEL, IN_WIDTH)), _const_spec((1, CHUNK_WIDTH))],
        out_specs=[full(nseq * SWA_HEADS * dec_seq, 2 * SWA_HEAD_DIM)]
        + [full(nseq * MEM_HEADS * dec_seq, MEM_HEAD_DIM)] * MEM_HEADS
        + [full(n, KV_WIDTH), full(n, KV_WIDTH), full(n, CHUNK_WIDTH), full(n, CHUNK_WIDTH)],
        out_shape=[jax.ShapeDtypeStruct((nseq * SWA_HEADS * dec_seq, 2 * SWA_HEAD_DIM), F32)]
        + [jax.ShapeDtypeStruct((nseq * MEM_HEADS * dec_seq, MEM_HEAD_DIM), F32)] * MEM_HEADS
        + [jax.ShapeDtypeStruct((n, KV_WIDTH), F32), jax.ShapeDtypeStruct((n, KV_WIDTH), F32),
           jax.ShapeDtypeStruct((n, CHUNK_WIDTH), F32), jax.ShapeDtypeStruct((n, CHUNK_WIDTH), BF16)],
        scratch_shapes=[pltpu.VMEM((n, 128), F32)],
        compiler_params=pltpu.CompilerParams(dimension_semantics=("arbitrary",), vmem_limit_bytes=VMEM_LIMIT),
        name="sample_proj",
    )(ws4, bs4, x2, g1, w_in_b, cng)


def _sample_attn_kernel(sinks_ref, qs_ref, qp0_ref, qp1_ref, qp2_ref, qp3_ref, kn_ref, vn_ref,
                        ck_ref, cv_ref, cmk_ref, cmv_ref,
                        aswa_ref, amem_ref, nk_ref, nv_ref, kk_ref, vv_ref, *, dec_seq):
    sb = ck_ref.shape[0]
    rows_s = SWA_HEADS * dec_seq
    rows_m = MEM_HEADS * dec_seq
    kk_ref[WINDOW:, :] = jnp.zeros((WINDOW, KV_WIDTH), F32)
    vv_ref[WINDOW:, :] = jnp.zeros((WINDOW, KV_WIDTH), F32)

    m_idx = lax.broadcasted_iota(jnp.int32, (rows_s, 2 * WINDOW), 0)
    c_idx = lax.broadcasted_iota(jnp.int32, (rows_s, 2 * WINDOW), 1)
    dist = WINDOW + (m_idx % dec_seq) - c_idx
    valid = (dist >= 0) & (dist <= WINDOW)
    head = m_idx // dec_seq
    slope_dist = jnp.exp2(-(head + 1).astype(F32)) * dist.astype(F32)
    head_col = lax.broadcasted_iota(jnp.int32, (rows_s, 1), 0) // dec_seq
    sink = jnp.zeros((rows_s, 1), F32)
    for hq in range(SWA_HEADS):
        sink = jnp.where(head_col == hq, sinks_ref[hq], sink)
    lane_kv = lax.broadcasted_iota(jnp.int32, (2 * WINDOW, KV_WIDTH), 1) // SWA_HEAD_DIM
    lane_s = lax.broadcasted_iota(jnp.int32, (dec_seq, SWA_WIDTH), 1) // SWA_HEAD_DIM
    lane_m = lax.broadcasted_iota(jnp.int32, (dec_seq, MEM_WIDTH), 1) // MEM_HEAD_DIM

    for sl in range(sb):
        tok = slice(sl * dec_seq, (sl + 1) * dec_seq)
        kk_ref[0:WINDOW, :] = ck_ref[sl]
        vv_ref[0:WINDOW, :] = cv_ref[sl]
        kk_ref[WINDOW:WINDOW + dec_seq, :] = kn_ref[tok, :]
        vv_ref[WINDOW:WINDOW + dec_seq, :] = vn_ref[tok, :]
        nk_ref[sl] = kk_ref[dec_seq:dec_seq + WINDOW, :]
        nv_ref[sl] = vv_ref[dec_seq:dec_seq + WINDOW, :]

        qs = qs_ref[sl * rows_s:(sl + 1) * rows_s, :].astype(BF16)
        s = _dot_nt(qs, kk_ref[...].astype(BF16)) * SWA_SCALE - slope_dist
        s = jnp.where(valid, s, NEG)
        m = jnp.maximum(jnp.max(s, axis=-1, keepdims=True), sink)
        e = jnp.exp(s - m)
        w = e / (jnp.sum(e, axis=-1, keepdims=True) + jnp.exp(sink - m))
        vv = vv_ref[...]
        vr = pltpu.roll(vv, SWA_HEAD_DIM, 1)
        dup0 = jnp.where(lane_kv == 0, vv, vr).astype(BF16)
        dup1 = jnp.where(lane_kv == 1, vv, vr).astype(BF16)
        o = _dot(w.astype(BF16), jnp.concatenate([dup0, dup0, dup1, dup1], axis=1))
        acc = jnp.zeros((dec_seq, SWA_WIDTH), F32)
        for hq in range(SWA_HEADS):
            acc = acc + jnp.where(lane_s == hq, o[hq * dec_seq:(hq + 1) * dec_seq, :], 0.0)
        aswa_ref[tok, :] = acc

        qp = jnp.concatenate([r[sl * rows_m:(sl + 1) * rows_m, :] for r in (qp0_ref, qp1_ref, qp2_ref, qp3_ref)],
                             axis=1).astype(BF16)
        wm = _softmax_rows(_dot_nt(qp, cmk_ref[sl].astype(BF16)) * MEM_SCALE)
        om = _dot(wm.astype(BF16), cmv_ref[sl].astype(BF16))
        accm = jnp.zeros((dec_seq, MEM_WIDTH), F32)
        for hm in range(MEM_HEADS):
            accm = accm + jnp.where(lane_m == hm, om[hm * dec_seq:(hm + 1) * dec_seq, :], 0.0)
        amem_ref[tok, :] = accm


def _sample_attn_call(sinks, qs, qps, kn, vn, ck, cv, cmk, cmv, *, dec_seq):
    nseq = ck.shape[0]
    sb = SB_SAMPLE
    rows = lambda r, w: pl.BlockSpec((sb * r, w), lambda i: (i, 0))
    seqs = lambda r, w: pl.BlockSpec((sb, r, w), lambda i: (i, 0, 0))
    n = nseq * dec_seq
    return pl.pallas_call(
        functools.partial(_sample_attn_kernel, dec_seq=dec_seq),
        grid=(nseq // sb,),
        in_specs=[_smem_spec(), rows(SWA_HEADS * dec_seq, 2 * SWA_HEAD_DIM)]
        + [rows(MEM_HEADS * dec_seq, MEM_HEAD_DIM)] * MEM_HEADS + [
                  rows(dec_seq, KV_WIDTH), rows(dec_seq, KV_WIDTH),
                  seqs(WINDOW, KV_WIDTH), seqs(WINDOW, KV_WIDTH), seqs(MEM_LEN, MEM_WIDTH), seqs(MEM_LEN, MEM_WIDTH)],
        out_specs=[rows(dec_seq, SWA_WIDTH), rows(dec_seq, MEM_WIDTH), seqs(WINDOW, KV_WIDTH), seqs(WINDOW, KV_WIDTH)],
        out_shape=[jax.ShapeDtypeStruct((n, SWA_WIDTH), F32), jax.ShapeDtypeStruct((n, MEM_WIDTH), F32),
                   jax.ShapeDtypeStruct((nseq, WINDOW, KV_WIDTH), F32),
                   jax.ShapeDtypeStruct((nseq, WINDOW, KV_WIDTH), F32)],
        scratch_shapes=[pltpu.VMEM((2 * WINDOW, KV_WIDTH), F32), pltpu.VMEM((2 * WINDOW, KV_WIDTH), F32)],
        compiler_params=pltpu.CompilerParams(dimension_semantics=("arbitrary",), vmem_limit_bytes=VMEM_LIMIT),
        name="sample_attn",
    )(sinks, qs, *qps, kn, vn, ck, cv, cmk, cmv)


def _sample_merge_kernel(x_ref, aswa_ref, achk_ref, amem_ref, g1_ref, win_ref, bg_ref, wbo_ref, wout_ref, g2_ref,
                         h_ref, xn2_ref):
    x = x_ref[...]
    xn = _rms(x, g1_ref[...]).astype(BF16)
    _merge_tail(x, xn, (aswa_ref[...].astype(BF16), achk_ref[...], amem_ref[...].astype(BF16)),
                win_ref, bg_ref, wbo_ref, wout_ref, g2_ref, h_ref, xn2_ref)


def _sample_merge_call(x2, aswa, achk, amem, g1, w_in_b, bg, wbo_b, wout_b, g2):
    n = x2.shape[0]
    full = lambda w: pl.BlockSpec((n, w), lambda i: (0, 0))
    return pl.pallas_call(
        _sample_merge_kernel,
        grid=(1,),
        in_specs=[full(D_MODEL), full(BRANCH_WIDTH), full(BRANCH_WIDTH), full(BRANCH_WIDTH),
                  _const_spec((1, D_MODEL)), _const_spec((D_MODEL, IN_WIDTH)), _const_spec((N_BRANCH, D_MODEL)),
                  _const_spec((N_BRANCH, BRANCH_WIDTH, D_MODEL)), _const_spec((D_MODEL, D_MODEL)),
                  _const_spec((1, D_MODEL))],
        out_specs=[full(D_MODEL), full(D_MODEL)],
        out_shape=[jax.ShapeDtypeStruct((n, D_MODEL), F32), jax.ShapeDtypeStruct((n, D_MODEL), BF16)],
        compiler_params=pltpu.CompilerParams(dimension_semantics=("arbitrary",), vmem_limit_bytes=VMEM_LIMIT),
        name="sample_merge",
    )(x2, aswa, achk, amem, g1, w_in_b, bg, wbo_b, wout_b, g2)


def _top16(s, key_iota):
    rank = jnp.full(s.shape, float(PEER_TOPK), F32)
    vals = []
    for kth in range(PEER_TOPK):
        m = jnp.max(s, axis=0, keepdims=True)
        idx = jnp.min(jnp.where(s == m, key_iota, float(PEER_N_KEYS)), axis=0, keepdims=True)
        hit = key_iota == idx
        rank = jnp.where(hit, float(kth), rank)
        s = jnp.where(hit, -jnp.inf, s)
        vals.append(m)
    return rank, vals


def _route_exact(s1, s2):
    lanes = s1.shape[1]
    key_iota = lax.broadcasted_iota(jnp.int32, (PEER_N_KEYS, lanes), 0).astype(F32)
    cand_iota = lax.broadcasted_iota(jnp.int32, (PEER_TOPK * PEER_TOPK, lanes), 0).astype(F32)
    k16 = lax.broadcasted_iota(jnp.int32, (PEER_TOPK, lanes), 0)
    rank1, v1 = _top16(s1, key_iota)
    rank2, v2 = _top16(s2, key_iota)
    top2 = jnp.zeros((PEER_TOPK, lanes), F32)
    for b in range(PEER_TOPK):
        top2 = jnp.where(k16 == b, v2[b], top2)
    cand = jnp.concatenate([v1[a] + top2 for a in range(PEER_TOPK)], axis=0)
    sel = jnp.zeros(cand.shape, F32)
    cmax = v1[0] + v2[0]
    z = jnp.zeros((1, lanes), F32)
    for _ in range(PEER_TOPK):
        m = jnp.max(cand, axis=0, keepdims=True)
        idx = jnp.min(jnp.where(cand == m, cand_iota, float(PEER_TOPK * PEER_TOPK)), axis=0, keepdims=True)
        hit = cand_iota == idx
        sel = jnp.where(hit, 1.0, sel)
        cand = jnp.where(hit, -jnp.inf, cand)
        z = z + jnp.exp(m - cmax)
    cnt = jnp.zeros((PEER_N_KEYS, lanes), F32)
    for a in range(PEER_TOPK):
        n_a = jnp.sum(sel[a * PEER_TOPK:(a + 1) * PEER_TOPK], axis=0, keepdims=True)
        cnt = jnp.where(rank1 == float(a), n_a, cnt)
    return cnt, rank2, jnp.exp(s1 - v1[0]), jnp.exp(s2 - v2[0]) / z


def _sort16_desc(v):
    n = len(v)
    k = 2
    while k <= n:
        j = k // 2
        while j >= 1:
            for i in range(n):
                l = i ^ j
                if l > i:
                    hi, lo = jnp.maximum(v[i], v[l]), jnp.minimum(v[i], v[l])
                    v[i], v[l] = (hi, lo) if (i & k) == 0 else (lo, hi)
            j //= 2
        k *= 2


def _merge16_desc(v):
    j = len(v) // 2
    while j >= 1:
        for i in range(len(v)):
            l = i ^ j
            if l > i:
                v[i], v[l] = jnp.maximum(v[i], v[l]), jnp.minimum(v[i], v[l])
        j //= 2


def _sublane_allsum(x):
    for shift in (4, 2, 1):
        x = x + pltpu.roll(x, shift, 0)
    return x


def _sorted_top16(tiles):
    v = list(tiles)
    _sort16_desc(v)
    for shift in (4, 2, 1):
        v = [jnp.maximum(v[r], pltpu.roll(v[PEER_TOPK - 1 - r], shift, 0)) for r in range(PEER_TOPK)]
        _merge16_desc(v)
    return v


def _route_fast(s1, s2):
    nt = PEER_N_KEYS // 8
    lanes = s1.shape[1]
    sub = lax.broadcasted_iota(jnp.int32, (8, lanes), 0)
    ninf = jnp.full((8, lanes), -jnp.inf, F32)
    zero = jnp.zeros((8, lanes), F32)
    t1 = [s1[r * 8:(r + 1) * 8, :] for r in range(nt)]
    t2 = [s2[r * 8:(r + 1) * 8, :] for r in range(nt)]
    v1 = _sorted_top16(t1)
    v2 = _sorted_top16(t2)

    rank2 = []
    for x in t2:
        r = jnp.full((8, lanes), float(PEER_TOPK), F32)
        for k in range(PEER_TOPK - 1, -1, -1):
            r = jnp.where(x >= v2[k], float(k), r)
        rank2.append(r)

    def rows(v, base):
        out = v[base]
        for b in range(1, 8):
            out = jnp.where(sub == b, v[base + b], out)
        return out

    r1 = [rows(v1, 0), rows(v1, 8)]
    r2 = [rows(v2, 0), rows(v2, 8)]
    cands = [v1[0] + r2[0], v1[0] + r2[1], v1[1] + r2[0],
             jnp.where(sub < 5, v1[2] + r2[0], ninf), jnp.where(sub < 4, v1[3] + r2[0], ninf),
             jnp.where(sub >= 4, v2[0] + r1[0], ninf), v2[0] + r1[1],
             jnp.where(sub >= 4, v2[1] + r1[0], ninf), jnp.where(sub == 4, v2[2] + r1[0], ninf)]
    top = _sorted_top16(cands + [ninf] * (PEER_TOPK - len(cands)))
    tau, cmax = top[PEER_TOPK - 1], top[0]
    z = zero
    for c in cands:
        z = z + jnp.where(c >= tau, jnp.exp(c - cmax), 0.0)
    z = _sublane_allsum(z)

    cnt = []
    for x in t1:
        c = zero
        for b in range(3):
            c = jnp.where(x + v2[b] >= tau, float(b + 1), c)
        cnt.append(c)
    for a in range(4):
        n_a = zero
        for b in range(PEER_TOPK // (a + 1)):
            n_a = jnp.where(v1[a] + v2[b] >= tau, float(b + 1), n_a)
        cnt = [jnp.where(x == v1[a], n_a, c) for x, c in zip(t1, cnt)]

    total = _sublane_allsum(functools.reduce(lambda a, b: a + b, cnt))
    ranked = _sublane_allsum(functools.reduce(lambda a, b: a + b,
                                              [jnp.where(r < float(PEER_TOPK), 1.0, 0.0) for r in rank2]))
    bad = jnp.where(total != float(PEER_TOPK), 1.0, 0.0) + jnp.where(ranked != float(PEER_TOPK), 1.0, 0.0)
    for k in range(PEER_TOPK - 1):
        bad = bad + jnp.where(v2[k] == v2[k + 1], 1.0, 0.0)
    for k in range(4):
        bad = bad + jnp.where(v1[k] == v1[k + 1], 1.0, 0.0)

    rz = 1.0 / z
    cat = lambda tiles: jnp.concatenate(tiles, axis=0)
    return (cat(cnt), cat(rank2), cat([jnp.exp(x - v1[0]) for x in t1]),
            cat([jnp.exp(x - v2[0]) * rz for x in t2]), bad)


def _peer_route_kernel(xn2_ref, wqt_ref, sub_ref, cnt_ref, rnk_ref, c1_ref, e2_ref, sc_s):
    tp = xn2_ref.shape[0]
    lanes = 128

    qt = _dot_nt(wqt_ref[...], xn2_ref[...]).astype(BF16)
    for hp in range(2 * PEER_HEADS):
        sc_s[hp] = _dot(sub_ref[hp % 2], qt[hp * PEER_HALF:(hp + 1) * PEER_HALF, :])

    def head_body(h, carry):
        for lg in range(tp // lanes):
            cols = slice(lg * lanes, (lg + 1) * lanes)
            s1 = sc_s[h * 2, :, cols]
            s2 = sc_s[h * 2 + 1, :, cols]

            def store(cnt, rank2, c1, e2):
                cnt_ref[h, :, cols] = cnt
                rnk_ref[h, :, cols] = rank2.astype(BF16)
                c1_ref[h, :, cols] = c1
                e2_ref[h, :, cols] = e2.astype(BF16)

            *tables, bad = _route_fast(s1, s2)
            store(*tables)

            @pl.when(jnp.max(bad) > 0.0)
            def _():
                store(*_route_exact(s1, s2))
        return carry

    lax.fori_loop(0, PEER_HEADS, head_body, 0)


def _peer_route_call(xn2, wqt_b, sub_b):
    n = xn2.shape[0]
    tp = TP_PRE
    tab = pl.BlockSpec((PEER_HEADS, PEER_N_KEYS, tp), lambda i: (0, 0, i))
    return pl.pallas_call(
        _peer_route_kernel,
        grid=(n // tp,),
        in_specs=[pl.BlockSpec((tp, D_MODEL), lambda i: (i, 0)),
                  _const_spec((PEER_HEADS * 2 * PEER_HALF, D_MODEL)), _const_spec((2, PEER_N_KEYS, PEER_HALF))],
        out_specs=[tab] * 4,
        out_shape=[jax.ShapeDtypeStruct((PEER_HEADS, PEER_N_KEYS, n), dt) for dt in (F32, BF16, F32, BF16)],
        scratch_shapes=[pltpu.VMEM((2 * PEER_HEADS, PEER_N_KEYS, tp), F32)],
        compiler_params=pltpu.CompilerParams(dimension_semantics=("arbitrary",), vmem_limit_bytes=VMEM_LIMIT),
        name="peer_route",
    )(xn2, wqt_b, sub_b)


def _peer_dense_kernel(xn2_ref, pu_ref, pvt_ref, cnt_ref, c1_ref, rnk_ref, e2_ref, h_ref, gf_ref,
                       y_ref, acc_ref, at_s, g_s, rnk_s, e2_s):
    k = pl.program_id(1)
    eb = at_s.shape[0]
    kps = eb // PEER_N_KEYS
    tm = xn2_ref.shape[0]
    sub_rows = 16
    ntile = PEER_N_KEYS // sub_rows

    @pl.when(k == 0)
    def _():
        acc_ref[...] = jnp.zeros_like(acc_ref)
        rnk_s[...] = rnk_ref[...]
        e2_s[...] = e2_ref[...]

    def weights_and_gate(row0):
        cnt_rows = [[cnt_ref[h, row0 + ii:row0 + ii + 1, :].astype(BF16) for ii in range(kps)]
                    for h in range(PEER_HEADS)]
        c1_rows = [[c1_ref[h, row0 + ii:row0 + ii + 1, :].astype(BF16) for ii in range(kps)]
                   for h in range(PEER_HEADS)]
        for cg in range(tm // 128):
            cols = slice(cg * 128, (cg + 1) * 128)
            w = [[None] * ntile for _ in range(kps)]
            for h in range(PEER_HEADS):
                cnt_b = [jnp.broadcast_to(cnt_rows[h][ii][:, cols], (sub_rows, 128)) for ii in range(kps)]
                c1_b = [jnp.broadcast_to(c1_rows[h][ii][:, cols], (sub_rows, 128)) for ii in range(kps)]
                for r in range(ntile):
                    jr = slice(r * sub_rows, (r + 1) * sub_rows)
                    rt = rnk_s[h, jr, cols]
                    et = e2_s[h, jr, cols]
                    for ii in range(kps):
                        term = jnp.where(rt < cnt_b[ii], et, 0.0) * c1_b[ii]
                        w[ii][r] = term if w[ii][r] is None else w[ii][r] + term
            for ii in range(kps):
                for r in range(ntile):
                    er = slice(ii * PEER_N_KEYS + r * sub_rows, ii * PEER_N_KEYS + (r + 1) * sub_rows)
                    g_s[er, cols] = _gelu(at_s[er, cols]).astype(BF16) * w[ii][r]

    xn2 = xn2_ref[...]
    for b in range(2):
        at_s[...] = _dot_nt(pu_ref[b * eb:(b + 1) * eb, :], xn2)
        weights_and_gate(b * kps)
        acc_ref[...] += _dot(pvt_ref[:, b * eb:(b + 1) * eb], g_s[...])

    @pl.when(k == pl.num_programs(1) - 1)
    def _():
        y_ref[...] = _rms(h_ref[...] + acc_ref[...].T, gf_ref[...])


def _peer_dense_call(xn2, pu_b, pvt_b, cnt, rnk, c1, e2, h, gf):
    n = xn2.shape[0]
    tm, eb = TM_DENSE, EB_DENSE
    npair = PEER_N_EXPERTS // (2 * eb)
    tok = lambda dt: pl.BlockSpec((tm, D_MODEL), lambda t, k: (t, 0))
    tab = pl.BlockSpec((PEER_HEADS, PEER_N_KEYS, tm), lambda t, k: (0, 0, t))
    rows = pl.BlockSpec((PEER_HEADS, 2 * eb // PEER_N_KEYS, tm), lambda t, k: (0, k, t))
    return pl.pallas_call(
        _peer_dense_kernel,
        grid=(n // tm, npair),
        in_specs=[tok(BF16), pl.BlockSpec((2 * eb, D_MODEL), lambda t, k: (k, 0)),
                  pl.BlockSpec((D_MODEL, 2 * eb), lambda t, k: (0, k)),
                  rows, rows, tab, tab, tok(F32), pl.BlockSpec((1, D_MODEL), lambda t, k: (0, 0))],
        out_specs=tok(F32),
        out_shape=jax.ShapeDtypeStruct((n, D_MODEL), F32),
        scratch_shapes=[pltpu.VMEM((D_MODEL, tm), F32), pltpu.VMEM((eb, tm), F32), pltpu.VMEM((eb, tm), BF16),
                        pltpu.VMEM((PEER_HEADS, PEER_N_KEYS, tm), BF16),
                        pltpu.VMEM((PEER_HEADS, PEER_N_KEYS, tm), BF16)],
        compiler_params=pltpu.CompilerParams(dimension_semantics=("arbitrary", "arbitrary"),
                                             vmem_limit_bytes=VMEM_LIMIT),
        name="peer_dense",
    )(xn2, pu_b, pvt_b, cnt, c1, rnk, e2, h, gf)


def _peer_and_final_norm(h, xn2, wqt_b, sub_b, pu_b, pvt_b, gf):
    cnt, rnk, c1, e2 = _peer_route_call(xn2, wqt_b, sub_b)
    return _peer_dense_call(xn2, pu_b, pvt_b, cnt, rnk, c1, e2, h, gf)


def kernel(x_prompt, x_sample, cache_swa_k, cache_swa_v, cache_mem_k, cache_mem_v, mem_prompt,
           norm1_g, w_in, b_gate, swa_sinks, chunk_norm_g, chunk_ws, chunk_bs, mem_norm_g, w_mem_kv,
           w_branch_out, w_out, norm2_g, peer_wq, peer_subkeys, peer_u, peer_v, final_norm_g):
    depth = norm1_g.shape[0]
    assert depth == 1, "single-layer step"
    batch, seq, _ = x_prompt.shape
    nseq, dec_seq, _ = x_sample.shape

    g1 = norm1_g[0][None]
    g2 = norm2_g[0][None]
    gf = final_norm_g[None]
    cng = chunk_norm_g[0][None]
    w_in_b = w_in[0].astype(BF16)
    wbo_b = w_branch_out[0].astype(BF16)
    wout_b = w_out[0].astype(BF16)
    wqt_b = peer_wq[0].T.astype(BF16)
    sub_b = peer_subkeys[0].astype(BF16)
    pu_b = peer_u[0].astype(BF16)
    pvt_b = peer_v[0].T.astype(BF16)
    sinks = swa_sinks[0]
    bst = chunk_bs[0].T
    ws4 = chunk_ws[0][:, :dec_seq, :dec_seq].reshape(-1)
    bs4 = chunk_bs[0][:, :dec_seq].reshape(-1)

    mk, mv, mkb, mvb = _memkv_call(mem_prompt, mem_norm_g[0][None], w_mem_kv[0].astype(BF16))
    hp, xn2p, kp, vp = _prompt_layer_call(
        x_prompt.reshape(batch * seq, D_MODEL), mkb, mvb, sinks, g1, w_in_b, cng, chunk_ws[0], bst,
        b_gate[0], wbo_b, wout_b, g2, batch=batch, seq=seq)
    y_prompt = _peer_and_final_norm(hp, xn2p, wqt_b, sub_b, pu_b, pvt_b, gf).reshape(batch, seq, D_MODEL)

    xs = x_sample.reshape(nseq * dec_seq, D_MODEL)
    qs, *qps, kn, vn, vcs, achk = _sample_proj_call(xs, ws4, bs4, g1, w_in_b, cng, dec_seq=dec_seq)
    aswa, amem, nk, nv = _sample_attn_call(
        sinks, qs, qps, kn, vn,
        cache_swa_k[0].reshape(nseq, WINDOW, KV_WIDTH), cache_swa_v[0].reshape(nseq, WINDOW, KV_WIDTH),
        cache_mem_k[0].reshape(nseq, MEM_LEN, MEM_WIDTH), cache_mem_v[0].reshape(nseq, MEM_LEN, MEM_WIDTH),
        dec_seq=dec_seq)
    hs, xn2s = _sample_merge_call(xs, aswa, achk, amem, g1, w_in_b, b_gate[0], wbo_b, wout_b, g2)
    y_sample = _peer_and_final_norm(hs, xn2s, wqt_b, sub_b, pu_b, pvt_b, gf).reshape(nseq, dec_seq, D_MODEL)

    kv5 = lambda a, b: a.reshape(1, b, WINDOW, SWA_KV_HEADS, SWA_HEAD_DIM)
    mem5 = lambda a: a.reshape(1, batch, MEM_LEN, MEM_HEADS, MEM_HEAD_DIM)
    return (y_prompt, y_sample, kv5(kp, batch), kv5(vp, batch), mem5(mk), mem5(mv),
            kv5(nk, nseq), kv5(nv, nseq), vcs.reshape(1, nseq, dec_seq, CHUNK_GROUPS, CHUNK_GROUP_DIM))
```

```python
import functools

import jax
import jax.numpy as jnp
from jax import lax
from jax.experimental import pallas as pl
from jax.experimental.pallas import tpu as pltpu

F32 = jnp.float32
BF16 = jnp.bfloat16

D_MODEL = 1024
SWA_HEADS = 8
SWA_KV_HEADS = 2
SWA_GROUP = SWA_HEADS // SWA_KV_HEADS
SWA_HEAD_DIM = 64
SWA_WIDTH = SWA_HEADS * SWA_HEAD_DIM
KV_WIDTH = SWA_KV_HEADS * SWA_HEAD_DIM
WINDOW = 128
SWA_SCALE = SWA_HEAD_DIM ** -0.5
CHUNK = 128
CHUNK_GROUPS = 4
CHUNK_GROUP_DIM = 128
CHUNK_WIDTH = CHUNK_GROUPS * CHUNK_GROUP_DIM
MEM_LEN = 256
MEM_HEADS = 4
MEM_HEAD_DIM = 128
MEM_WIDTH = MEM_HEADS * MEM_HEAD_DIM
MEM_SCALE = MEM_HEAD_DIM ** -0.5
N_BRANCH = 3
BRANCH_WIDTH = 512
PEER_HEADS = 8
PEER_N_KEYS = 128
PEER_N_EXPERTS = PEER_N_KEYS * PEER_N_KEYS
PEER_HALF = 128
PEER_TOPK = 16
EPS = 1e-6
NEG = -1e30

O_Q = 0
O_K = O_Q + SWA_WIDTH
O_V = O_K + KV_WIDTH
O_UV = O_V + KV_WIDTH
O_QM = O_UV + 2 * CHUNK_WIDTH
O_GL = O_QM + MEM_WIDTH
IN_WIDTH = O_GL + N_BRANCH * D_MODEL

TM_LAYER = 512
TP_PRE = 256
TM_DENSE = 512
EB_DENSE = 512
SB_SAMPLE = 8
VMEM_LIMIT = 56 * 1024 * 1024

_NT = (((1,), (1,)), ((), ()))


def _rms(x, g):
    return x * lax.rsqrt(jnp.mean(x * x, axis=-1, keepdims=True) + EPS) * g


def _gelu(x):
    return 0.5 * x * (1.0 + lax.erf(x * (2.0 ** -0.5)))


def _dot(a, b):
    return jnp.dot(a, b, preferred_element_type=F32)


def _dot_nt(a, b):
    return lax.dot_general(a, b, _NT, preferred_element_type=F32)


def _const_spec(shape):
    nd = len(shape)
    return pl.BlockSpec(shape, lambda *_: (0,) * nd, pipeline_mode=pl.Buffered(1))


def _smem_spec():
    return pl.BlockSpec(memory_space=pltpu.SMEM)


def _chunk_uv(xn, win_ref, cng_ref):
    uv = _gelu(_dot(xn, win_ref[:, O_UV:O_QM]))
    return uv[:, :CHUNK_WIDTH], _rms(uv[:, CHUNK_WIDTH:], cng_ref[...])


def _merge_tail(x, xn, branches, win_ref, bg_ref, wbo_ref, wout_ref, g2_ref, h_ref, xn2_ref):
    acc = None
    for j in range(N_BRANCH):
        pj = _dot(branches[j], wbo_ref[j])
        gl = _dot(xn, win_ref[:, O_GL + j * D_MODEL:O_GL + (j + 1) * D_MODEL]) + bg_ref[j:j + 1, :]
        term = jax.nn.sigmoid(gl) * pj
        acc = term if acc is None else acc + term
    h = x + _dot(acc.astype(BF16), wout_ref[...])
    h_ref[...] = h
    xn2_ref[...] = _rms(h, g2_ref[...]).astype(BF16)


def _softmax_rows(s):
    m = jnp.max(s, axis=-1, keepdims=True)
    e = jnp.exp(s - m)
    return e / jnp.sum(e, axis=-1, keepdims=True)


def _memkv_kernel(mem_ref, g_ref, w_ref, k_ref, v_ref, kb_ref, vb_ref):
    kv = _dot(_rms(mem_ref[...], g_ref[...]).astype(BF16), w_ref[...])
    k = kv[:, :MEM_WIDTH]
    v = kv[:, MEM_WIDTH:]
    k_ref[...] = k
    v_ref[...] = v
    kb_ref[...] = k.astype(BF16)
    vb_ref[...] = v.astype(BF16)


def _memkv_call(mem, g, w_b):
    b, m, d = mem.shape
    blk = lambda w: pl.BlockSpec((None, m, w), lambda i: (i, 0, 0))
    return pl.pallas_call(
        _memkv_kernel,
        grid=(b,),
        in_specs=[blk(d), _const_spec((1, d)), _const_spec((d, 2 * MEM_WIDTH))],
        out_specs=[blk(MEM_WIDTH)] * 4,
        out_shape=[jax.ShapeDtypeStruct((b, m, MEM_WIDTH), F32)] * 2
        + [jax.ShapeDtypeStruct((b, m, MEM_WIDTH), BF16)] * 2,
        compiler_params=pltpu.CompilerParams(dimension_semantics=("arbitrary",), vmem_limit_bytes=VMEM_LIMIT),
        name="mem_kv",
    )(mem, g, w_b)


def _prompt_layer_kernel(sinks_ref, x_ref, mk_ref, mv_ref, g1_ref, win_ref, cng_ref, ws_ref, bst_ref,
                         bg_ref, wbo_ref, wout_ref, g2_ref,
                         h_ref, xn2_ref, kout_ref, vout_ref,
                         kprev_ref, vprev_ref, aswa_ref, achk_ref, amem_ref, *, tiles_per_batch):
    tm = x_ref.shape[0]
    nblk = tm // WINDOW
    i = pl.program_id(0)
    first = (i % tiles_per_batch) == 0

    @pl.when(first)
    def _():
        kprev_ref[...] = jnp.zeros_like(kprev_ref)
        vprev_ref[...] = jnp.zeros_like(vprev_ref)

    x = x_ref[...]
    xn = _rms(x, g1_ref[...]).astype(BF16)

    q = _dot(xn, win_ref[:, O_Q:O_K]).astype(BF16)
    k = _dot(xn, win_ref[:, O_K:O_V])
    v = _dot(xn, win_ref[:, O_V:O_UV])
    kout_ref[...] = k[tm - WINDOW:]
    vout_ref[...] = v[tm - WINDOW:]
    kb = k.astype(BF16)
    vb = v.astype(BF16)
    r = lax.broadcasted_iota(jnp.int32, (WINDOW, 2 * WINDOW), 0)
    c = lax.broadcasted_iota(jnp.int32, (WINDOW, 2 * WINDOW), 1)
    dist = WINDOW + r - c
    band = (dist >= 0) & (dist <= WINDOW)
    distf = dist.astype(F32)
    first_lo = jnp.where(first, WINDOW, 0)
    for n in range(nblk):
        rows = slice(n * WINDOW, (n + 1) * WINDOW)
        if n == 0:
            kp, vp = kprev_ref[...], vprev_ref[...]
            valid = band & (c >= first_lo)
        else:
            prev = slice((n - 1) * WINDOW, n * WINDOW)
            kp, vp = kb[prev], vb[prev]
            valid = band
        kk = jnp.concatenate([kp, kb[rows]], axis=0)
        vv = jnp.concatenate([vp, vb[rows]], axis=0)
        for hq in range(SWA_HEADS):
            kvs = slice((hq // SWA_GROUP) * SWA_HEAD_DIM, (hq // SWA_GROUP + 1) * SWA_HEAD_DIM)
            hs = slice(hq * SWA_HEAD_DIM, (hq + 1) * SWA_HEAD_DIM)
            s = _dot_nt(q[rows, hs], kk[:, kvs]) * SWA_SCALE - (2.0 ** -(hq + 1)) * distf
            s = jnp.where(valid, s, NEG)
            sink = sinks_ref[hq]
            m = jnp.maximum(jnp.max(s, axis=-1, keepdims=True), sink)
            e = jnp.exp(s - m)
            w = e / (jnp.sum(e, axis=-1, keepdims=True) + jnp.exp(sink - m))
            aswa_ref[rows, hs] = _dot(w.astype(BF16), vv[:, kvs]).astype(BF16)
    kprev_ref[...] = kb[tm - WINDOW:]
    vprev_ref[...] = vb[tm - WINDOW:]

    u, vc = _chunk_uv(xn, win_ref, cng_ref)
    vcb = vc.astype(BF16)
    tri = (lax.broadcasted_iota(jnp.int32, (CHUNK, CHUNK), 0)
           >= lax.broadcasted_iota(jnp.int32, (CHUNK, CHUNK), 1))
    for g in range(CHUNK_GROUPS):
        gs = slice(g * CHUNK_GROUP_DIM, (g + 1) * CHUNK_GROUP_DIM)
        wg = jnp.where(tri, ws_ref[g], 0.0).astype(BF16)
        for n in range(nblk):
            rows = slice(n * CHUNK, (n + 1) * CHUNK)
            sg = _dot(wg, vcb[rows, gs]) + bst_ref[:, g:g + 1]
            achk_ref[rows, gs] = (u[rows, gs] * sg).astype(BF16)

    qm = _dot(xn, win_ref[:, O_QM:O_GL]).astype(BF16)
    for hm in range(MEM_HEADS):
        hs = slice(hm * MEM_HEAD_DIM, (hm + 1) * MEM_HEAD_DIM)
        w = _softmax_rows(_dot_nt(qm[:, hs], mk_ref[:, hs]) * MEM_SCALE)
        amem_ref[:, hs] = _dot(w.astype(BF16), mv_ref[:, hs]).astype(BF16)

    _merge_tail(x, xn, (aswa_ref[...], achk_ref[...], amem_ref[...]),
                win_ref, bg_ref, wbo_ref, wout_ref, g2_ref, h_ref, xn2_ref)


def _prompt_layer_call(x2, mkb, mvb, sinks, g1, w_in_b, cng, ws, bst, bg, wbo_b, wout_b, g2, *, batch, seq):
    n = x2.shape[0]
    tm = TM_LAYER
    tpb = seq // tm
    tile = lambda w: pl.BlockSpec((tm, w), lambda i: (i, 0))
    per_batch = lambda r, w: pl.BlockSpec((None, r, w), lambda i: (i // tpb, 0, 0))
    return pl.pallas_call(
        functools.partial(_prompt_layer_kernel, tiles_per_batch=tpb),
        grid=(n // tm,),
        in_specs=[_smem_spec(), tile(D_MODEL), per_batch(MEM_LEN, MEM_WIDTH), per_batch(MEM_LEN, MEM_WIDTH),
                  _const_spec((1, D_MODEL)), _const_spec((D_MODEL, IN_WIDTH)), _const_spec((1, CHUNK_WIDTH)),
                  _const_spec((CHUNK_GROUPS, CHUNK, CHUNK)), _const_spec((CHUNK, CHUNK_GROUPS)),
                  _const_spec((N_BRANCH, D_MODEL)), _const_spec((N_BRANCH, BRANCH_WIDTH, D_MODEL)),
                  _const_spec((D_MODEL, D_MODEL)), _const_spec((1, D_MODEL))],
        out_specs=[tile(D_MODEL), tile(D_MODEL), per_batch(WINDOW, KV_WIDTH), per_batch(WINDOW, KV_WIDTH)],
        out_shape=[jax.ShapeDtypeStruct((n, D_MODEL), F32), jax.ShapeDtypeStruct((n, D_MODEL), BF16),
                   jax.ShapeDtypeStruct((batch, WINDOW, KV_WIDTH), F32),
                   jax.ShapeDtypeStruct((batch, WINDOW, KV_WIDTH), F32)],
        scratch_shapes=[pltpu.VMEM((WINDOW, KV_WIDTH), BF16), pltpu.VMEM((WINDOW, KV_WIDTH), BF16),
                        pltpu.VMEM((tm, BRANCH_WIDTH), BF16), pltpu.VMEM((tm, BRANCH_WIDTH), BF16),
                        pltpu.VMEM((tm, BRANCH_WIDTH), BF16)],
        compiler_params=pltpu.CompilerParams(dimension_semantics=("arbitrary",), vmem_limit_bytes=VMEM_LIMIT),
        name="prompt_layer",
    )(sinks, x2, mkb, mvb, g1, w_in_b, cng, ws, bst, bg, wbo_b, wout_b, g2)


def _sample_proj_kernel(ws4_ref, bs4_ref, x_ref, g1_ref, win_ref, cng_ref,
                        qs_ref, qp0_ref, qp1_ref, qp2_ref, qp3_ref, k_ref, v_ref, vc_ref, achk_ref, qtmp_ref,
                        *, dec_seq):
    n = x_ref.shape[0]
    nseq = n // dec_seq
    xn = _rms(x_ref[...], g1_ref[...]).astype(BF16)
    q = _dot(xn, win_ref[:, O_Q:O_K])
    k_ref[...] = _dot(xn, win_ref[:, O_K:O_V])
    v_ref[...] = _dot(xn, win_ref[:, O_V:O_UV])
    qm = _dot(xn, win_ref[:, O_QM:O_GL])

    lane = lax.broadcasted_iota(jnp.int32, (n, 2 * SWA_HEAD_DIM), 1)
    for hq in range(SWA_HEADS):
        kvh = hq // SWA_GROUP
        win = q[:, (hq // 2) * 128:(hq // 2 + 1) * 128]
        if (hq % 2) != kvh:
            win = pltpu.roll(win, SWA_HEAD_DIM, 1)
        qtmp_ref[...] = jnp.where((lane // SWA_HEAD_DIM) == kvh, win, 0.0)
        for t in range(dec_seq):
            qs_ref[pl.ds(hq * dec_seq + t, nseq, stride=SWA_HEADS * dec_seq), :] = (
                qtmp_ref[pl.ds(t, nseq, stride=dec_seq), :])
    for hm, qp_ref in enumerate((qp0_ref, qp1_ref, qp2_ref, qp3_ref)):
        qp_ref[...] = jnp.zeros_like(qp_ref)
        qtmp_ref[...] = qm[:, hm * MEM_HEAD_DIM:(hm + 1) * MEM_HEAD_DIM]
        for t in range(dec_seq):
            qp_ref[pl.ds(hm * dec_seq + t, nseq, stride=MEM_HEADS * dec_seq), :] = (
                qtmp_ref[pl.ds(t, nseq, stride=dec_seq), :])

    u, vc = _chunk_uv(xn, win_ref, cng_ref)
    vc_ref[...] = vc
    tpos = lax.broadcasted_iota(jnp.int32, (n, CHUNK_GROUP_DIM), 0) % dec_seq
    for g in range(CHUNK_GROUPS):
        gs = slice(g * CHUNK_GROUP_DIM, (g + 1) * CHUNK_GROUP_DIM)
        vg = vc[:, gs]
        sg = jnp.zeros((n, CHUNK_GROUP_DIM), F32)
        for t in range(dec_seq):
            sg = jnp.where(tpos == t, bs4_ref[g * dec_seq + t], sg)
        for d in range(dec_seq):
            shifted = vg if d == 0 else pltpu.roll(vg, d, 0)
            coef = jnp.zeros((n, CHUNK_GROUP_DIM), F32)
            for t in range(d, dec_seq):
                coef = jnp.where(tpos == t, ws4_ref[(g * dec_seq + t) * dec_seq + (t - d)], coef)
            sg = sg + coef * shifted
        achk_ref[:, gs] = (u[:, gs] * sg).astype(BF16)


def _sample_proj_call(x2, ws4, bs4, g1, w_in_b, cng, *, dec_seq):
    n = x2.shape[0]
    nseq = n // dec_seq
    full = lambda r, w: pl.BlockSpec((r, w), lambda i: (0, 0))
    return pl.pallas_call(
        functools.partial(_sample_proj_kernel, dec_seq=dec_seq),
        grid=(1,),
        in_specs=[_smem_spec(), _smem_spec(), full(n, D_MODEL), _const_spec((1, D_MODEL)),
                  _const_spec((D_MODEL, IN_WIDTH)), _const_spec((1, CHUNK_WIDTH))],
        out_specs=[full(nseq * SWA_HEADS * dec_seq, 2 * SWA_HEAD_DIM)]
        + [full(nseq * MEM_HEADS * dec_seq, MEM_HEAD_DIM)] * MEM_HEADS
        + [full(n, KV_WIDTH), full(n, KV_WIDTH), full(n, CHUNK_WIDTH), full(n, CHUNK_WIDTH)],
        out_shape=[jax.ShapeDtypeStruct((nseq * SWA_HEADS * dec_seq, 2 * SWA_HEAD_DIM), F32)]
        + [jax.ShapeDtypeStruct((nseq * MEM_HEADS * dec_seq, MEM_HEAD_DIM), F32)] * MEM_HEADS
        + [jax.ShapeDtypeStruct((n, KV_WIDTH), F32), jax.ShapeDtypeStruct((n, KV_WIDTH), F32),
           jax.ShapeDtypeStruct((n, CHUNK_WIDTH), F32), jax.ShapeDtypeStruct((n, CHUNK_WIDTH), BF16)],
        scratch_shapes=[pltpu.VMEM((n, 128), F32)],
        compiler_params=pltpu.CompilerParams(dimension_semantics=("arbitrary",), vmem_limit_bytes=VMEM_LIMIT),
        name="sample_proj",
    )(ws4, bs4, x2, g1, w_in_b, cng)


def _sample_attn_kernel(sinks_ref, qs_ref, qp0_ref, qp1_ref, qp2_ref, qp3_ref, kn_ref, vn_ref,
                        ck_ref, cv_ref, cmk_ref, cmv_ref,
                        aswa_ref, amem_ref, nk_ref, nv_ref, kk_ref, vv_ref, *, dec_seq):
    sb = ck_ref.shape[0]
    rows_s = SWA_HEADS * dec_seq
    rows_m = MEM_HEADS * dec_seq
    kk_ref[:, WINDOW:, :] = jnp.zeros((sb, WINDOW, KV_WIDTH), F32)
    vv_ref[:, WINDOW:, :] = jnp.zeros((sb, WINDOW, KV_WIDTH), F32)

    m_idx = lax.broadcasted_iota(jnp.int32, (rows_s, 2 * WINDOW), 0)
    c_idx = lax.broadcasted_iota(jnp.int32, (rows_s, 2 * WINDOW), 1)
    dist = WINDOW + (m_idx % dec_seq) - c_idx
    valid = (dist >= 0) & (dist <= WINDOW)
    head = m_idx // dec_seq
    slope_dist = jnp.exp2(-(head + 1).astype(F32)) * dist.astype(F32)
    head_col = lax.broadcasted_iota(jnp.int32, (rows_s, 1), 0) // dec_seq
    sink = jnp.zeros((rows_s, 1), F32)
    for hq in range(SWA_HEADS):
        sink = jnp.where(head_col == hq, sinks_ref[hq], sink)
    lane_kv = lax.broadcasted_iota(jnp.int32, (2 * WINDOW, KV_WIDTH), 1) // SWA_HEAD_DIM
    lane_s = lax.broadcasted_iota(jnp.int32, (dec_seq, SWA_WIDTH), 1) // SWA_HEAD_DIM
    lane_m = lax.broadcasted_iota(jnp.int32, (dec_seq, MEM_WIDTH), 1) // MEM_HEAD_DIM

    for sl in range(sb):
        tok = slice(sl * dec_seq, (sl + 1) * dec_seq)
        kk_ref[sl, 0:WINDOW, :] = ck_ref[sl]
        vv_ref[sl, 0:WINDOW, :] = cv_ref[sl]
        kk_ref[sl, WINDOW:WINDOW + dec_seq, :] = kn_ref[tok, :]
        vv_ref[sl, WINDOW:WINDOW + dec_seq, :] = vn_ref[tok, :]
        nk_ref[sl] = kk_ref[sl, dec_seq:dec_seq + WINDOW, :]
        nv_ref[sl] = vv_ref[sl, dec_seq:dec_seq + WINDOW, :]

        qs = qs_ref[sl * rows_s:(sl + 1) * rows_s, :].astype(BF16)
        s = _dot_nt(qs, kk_ref[sl].astype(BF16)) * SWA_SCALE - slope_dist
        s = jnp.where(valid, s, NEG)
        m = jnp.maximum(jnp.max(s, axis=-1, keepdims=True), sink)
        e = jnp.exp(s - m)
        w = e / (jnp.sum(e, axis=-1, keepdims=True) + jnp.exp(sink - m))
        vv = vv_ref[sl]
        vr = pltpu.roll(vv, SWA_HEAD_DIM, 1)
        dup0 = jnp.where(lane_kv == 0, vv, vr).astype(BF16)
        dup1 = jnp.where(lane_kv == 1, vv, vr).astype(BF16)
        o = _dot(w.astype(BF16), jnp.concatenate([dup0, dup0, dup1, dup1], axis=1))
        acc = jnp.zeros((dec_seq, SWA_WIDTH), F32)
        for hq in range(SWA_HEADS):
            acc = acc + jnp.where(lane_s == hq, o[hq * dec_seq:(hq + 1) * dec_seq, :], 0.0)
        aswa_ref[tok, :] = acc

        qp = jnp.concatenate([r[sl * rows_m:(sl + 1) * rows_m, :] for r in (qp0_ref, qp1_ref, qp2_ref, qp3_ref)],
                             axis=1).astype(BF16)
        mem_k, mem_v = (jnp.concatenate([ref[sl, pl.ds(hm, MEM_LEN, stride=MEM_HEADS), :]
                                         for hm in range(MEM_HEADS)], axis=1).astype(BF16)
                        for ref in (cmk_ref, cmv_ref))
        wm = _softmax_rows(_dot_nt(qp, mem_k) * MEM_SCALE)
        om = _dot(wm.astype(BF16), mem_v)
        accm = jnp.zeros((dec_seq, MEM_WIDTH), F32)
        for hm in range(MEM_HEADS):
            accm = accm + jnp.where(lane_m == hm, om[hm * dec_seq:(hm + 1) * dec_seq, :], 0.0)
        amem_ref[tok, :] = accm


def _sample_attn_call(sinks, qs, qps, kn, vn, ck, cv, cmk, cmv, *, dec_seq):
    nseq = ck.shape[0]
    sb = SB_SAMPLE
    rows = lambda r, w: pl.BlockSpec((sb * r, w), lambda i: (i, 0))
    seqs = lambda r, w: pl.BlockSpec((sb, r, w), lambda i: (i, 0, 0))
    n = nseq * dec_seq
    return pl.pallas_call(
        functools.partial(_sample_attn_kernel, dec_seq=dec_seq),
        grid=(nseq // sb,),
        in_specs=[_smem_spec(), rows(SWA_HEADS * dec_seq, 2 * SWA_HEAD_DIM)]
        + [rows(MEM_HEADS * dec_seq, MEM_HEAD_DIM)] * MEM_HEADS + [
                  rows(dec_seq, KV_WIDTH), rows(dec_seq, KV_WIDTH),
                  seqs(WINDOW, KV_WIDTH), seqs(WINDOW, KV_WIDTH),
                  seqs(MEM_LEN * MEM_HEADS, MEM_HEAD_DIM), seqs(MEM_LEN * MEM_HEADS, MEM_HEAD_DIM)],
        out_specs=[rows(dec_seq, SWA_WIDTH), rows(dec_seq, MEM_WIDTH), seqs(WINDOW, KV_WIDTH), seqs(WINDOW, KV_WIDTH)],
        out_shape=[jax.ShapeDtypeStruct((n, SWA_WIDTH), F32), jax.ShapeDtypeStruct((n, MEM_WIDTH), F32),
                   jax.ShapeDtypeStruct((nseq, WINDOW, KV_WIDTH), F32),
                   jax.ShapeDtypeStruct((nseq, WINDOW, KV_WIDTH), F32)],
        scratch_shapes=[pltpu.VMEM((sb, 2 * WINDOW, KV_WIDTH), F32), pltpu.VMEM((sb, 2 * WINDOW, KV_WIDTH), F32)],
        compiler_params=pltpu.CompilerParams(dimension_semantics=("arbitrary",), vmem_limit_bytes=VMEM_LIMIT),
        name="sample_attn",
    )(sinks, qs, *qps, kn, vn, ck, cv, cmk, cmv)


def _sample_merge_kernel(x_ref, aswa_ref, achk_ref, amem_ref, g1_ref, win_ref, bg_ref, wbo_ref, wout_ref, g2_ref,
                         h_ref, xn2_ref):
    x = x_ref[...]
    xn = _rms(x, g1_ref[...]).astype(BF16)
    _merge_tail(x, xn, (aswa_ref[...].astype(BF16), achk_ref[...], amem_ref[...].astype(BF16)),
                win_ref, bg_ref, wbo_ref, wout_ref, g2_ref, h_ref, xn2_ref)


def _sample_merge_call(x2, aswa, achk, amem, g1, w_in_b, bg, wbo_b, wout_b, g2):
    n = x2.shape[0]
    full = lambda w: pl.BlockSpec((n, w), lambda i: (0, 0))
    return pl.pallas_call(
        _sample_merge_kernel,
        grid=(1,),
        in_specs=[full(D_MODEL), full(BRANCH_WIDTH), full(BRANCH_WIDTH), full(BRANCH_WIDTH),
                  _const_spec((1, D_MODEL)), _const_spec((D_MODEL, IN_WIDTH)), _const_spec((N_BRANCH, D_MODEL)),
                  _const_spec((N_BRANCH, BRANCH_WIDTH, D_MODEL)), _const_spec((D_MODEL, D_MODEL)),
                  _const_spec((1, D_MODEL))],
        out_specs=[full(D_MODEL), full(D_MODEL)],
        out_shape=[jax.ShapeDtypeStruct((n, D_MODEL), F32), jax.ShapeDtypeStruct((n, D_MODEL), BF16)],
        compiler_params=pltpu.CompilerParams(dimension_semantics=("arbitrary",), vmem_limit_bytes=VMEM_LIMIT),
        name="sample_merge",
    )(x2, aswa, achk, amem, g1, w_in_b, bg, wbo_b, wout_b, g2)


def _expert_tables_kernel(pu_ref, pv_ref, pub_ref, pvt_ref):
    pub_ref[...] = pu_ref[...].astype(BF16)
    pvt_ref[...] = pv_ref[...].T.astype(BF16)


def _expert_tables_call(peer_u, peer_v):
    _, ne, d = peer_u.shape
    rows = 2 * EB_DENSE
    blk = pl.BlockSpec((None, rows, d), lambda i: (0, i, 0))
    return pl.pallas_call(
        _expert_tables_kernel,
        grid=(ne // rows,),
        in_specs=[blk, blk],
        out_specs=[pl.BlockSpec((rows, d), lambda i: (i, 0)), pl.BlockSpec((None, d, rows), lambda i: (i, 0, 0))],
        out_shape=[jax.ShapeDtypeStruct((ne, d), BF16), jax.ShapeDtypeStruct((ne // rows, d, rows), BF16)],
        compiler_params=pltpu.CompilerParams(dimension_semantics=("arbitrary",), vmem_limit_bytes=VMEM_LIMIT),
        name="expert_tables",
    )(peer_u, peer_v)


def _top16(s, key_iota):
    rank = jnp.full(s.shape, float(PEER_TOPK), F32)
    vals = []
    for kth in range(PEER_TOPK):
        m = jnp.max(s, axis=0, keepdims=True)
        idx = jnp.min(jnp.where(s == m, key_iota, float(PEER_N_KEYS)), axis=0, keepdims=True)
        hit = key_iota == idx
        rank = jnp.where(hit, float(kth), rank)
        s = jnp.where(hit, -jnp.inf, s)
        vals.append(m)
    return rank, vals


def _route_exact(s1, s2):
    lanes = s1.shape[1]
    key_iota = lax.broadcasted_iota(jnp.int32, (PEER_N_KEYS, lanes), 0).astype(F32)
    cand_iota = lax.broadcasted_iota(jnp.int32, (PEER_TOPK * PEER_TOPK, lanes), 0).astype(F32)
    k16 = lax.broadcasted_iota(jnp.int32, (PEER_TOPK, lanes), 0)
    rank1, v1 = _top16(s1, key_iota)
    rank2, v2 = _top16(s2, key_iota)
    top2 = jnp.zeros((PEER_TOPK, lanes), F32)
    for b in range(PEER_TOPK):
        top2 = jnp.where(k16 == b, v2[b], top2)
    cand = jnp.concatenate([v1[a] + top2 for a in range(PEER_TOPK)], axis=0)
    sel = jnp.zeros(cand.shape, F32)
    cmax = v1[0] + v2[0]
    z = jnp.zeros((1, lanes), F32)
    for _ in range(PEER_TOPK):
        m = jnp.max(cand, axis=0, keepdims=True)
        idx = jnp.min(jnp.where(cand == m, cand_iota, float(PEER_TOPK * PEER_TOPK)), axis=0, keepdims=True)
        hit = cand_iota == idx
        sel = jnp.where(hit, 1.0, sel)
        cand = jnp.where(hit, -jnp.inf, cand)
        z = z + jnp.exp(m - cmax)
    cnt = jnp.zeros((PEER_N_KEYS, lanes), F32)
    for a in range(PEER_TOPK):
        n_a = jnp.sum(sel[a * PEER_TOPK:(a + 1) * PEER_TOPK], axis=0, keepdims=True)
        cnt = jnp.where(rank1 == float(a), n_a, cnt)
    return cnt, rank2, jnp.exp(s1 - v1[0]), jnp.exp(s2 - v2[0]) / z


def _sort16_desc(v):
    n = len(v)
    k = 2
    while k <= n:
        j = k // 2
        while j >= 1:
            for i in range(n):
                l = i ^ j
                if l > i:
                    hi, lo = jnp.maximum(v[i], v[l]), jnp.minimum(v[i], v[l])
                    v[i], v[l] = (hi, lo) if (i & k) == 0 else (lo, hi)
            j //= 2
        k *= 2


def _merge16_desc(v):
    j = len(v) // 2
    while j >= 1:
        for i in range(len(v)):
            l = i ^ j
            if l > i:
                v[i], v[l] = jnp.maximum(v[i], v[l]), jnp.minimum(v[i], v[l])
        j //= 2


def _sublane_allsum(x):
    for shift in (4, 2, 1):
        x = x + pltpu.roll(x, shift, 0)
    return x


def _sorted_top16(tiles):
    v = list(tiles)
    _sort16_desc(v)
    for shift in (4, 2, 1):
        v = [jnp.maximum(v[r], pltpu.roll(v[PEER_TOPK - 1 - r], shift, 0)) for r in range(PEER_TOPK)]
        _merge16_desc(v)
    return v


def _route_fast(s1, s2):
    nt = PEER_N_KEYS // 8
    lanes = s1.shape[1]
    sub = lax.broadcasted_iota(jnp.int32, (8, lanes), 0)
    ninf = jnp.full((8, lanes), -jnp.inf, F32)
    zero = jnp.zeros((8, lanes), F32)
    t1 = [s1[r * 8:(r + 1) * 8, :] for r in range(nt)]
    t2 = [s2[r * 8:(r + 1) * 8, :] for r in range(nt)]
    v1 = _sorted_top16(t1)
    v2 = _sorted_top16(t2)

    rank2 = []
    for x in t2:
        r = jnp.full((8, lanes), float(PEER_TOPK), F32)
        for k in range(PEER_TOPK - 1, -1, -1):
            r = jnp.where(x >= v2[k], float(k), r)
        rank2.append(r)

    def rows(v, base):
        out = v[base]
        for b in range(1, 8):
            out = jnp.where(sub == b, v[base + b], out)
        return out

    r1 = [rows(v1, 0), rows(v1, 8)]
    r2 = [rows(v2, 0), rows(v2, 8)]
    cands = [v1[0] + r2[0], v1[0] + r2[1], v1[1] + r2[0],
             jnp.where(sub < 5, v1[2] + r2[0], ninf), jnp.where(sub < 4, v1[3] + r2[0], ninf),
             jnp.where(sub >= 4, v2[0] + r1[0], ninf), v2[0] + r1[1],
             jnp.where(sub >= 4, v2[1] + r1[0], ninf), jnp.where(sub == 4, v2[2] + r1[0], ninf)]
    top = _sorted_top16(cands + [ninf] * (PEER_TOPK - len(cands)))
    tau, cmax = top[PEER_TOPK - 1], top[0]
    z = zero
    for c in cands:
        z = z + jnp.where(c >= tau, jnp.exp(c - cmax), 0.0)
    z = _sublane_allsum(z)

    cnt = []
    for x in t1:
        c = zero
        for b in range(3):
            c = jnp.where(x + v2[b] >= tau, float(b + 1), c)
        cnt.append(c)
    for a in range(4):
        n_a = zero
        for b in range(PEER_TOPK // (a + 1)):
            n_a = jnp.where(v1[a] + v2[b] >= tau, float(b + 1), n_a)
        cnt = [jnp.where(x == v1[a], n_a, c) for x, c in zip(t1, cnt)]

    total = _sublane_allsum(functools.reduce(lambda a, b: a + b, cnt))
    ranked = _sublane_allsum(functools.reduce(lambda a, b: a + b,
                                              [jnp.where(r < float(PEER_TOPK), 1.0, 0.0) for r in rank2]))
    bad = jnp.where(total != float(PEER_TOPK), 1.0, 0.0) + jnp.where(ranked != float(PEER_TOPK), 1.0, 0.0)
    for k in range(PEER_TOPK - 1):
        bad = bad + jnp.where(v2[k] == v2[k + 1], 1.0, 0.0)
    for k in range(4):
        bad = bad + jnp.where(v1[k] == v1[k + 1], 1.0, 0.0)

    rz = 1.0 / z
    cat = lambda tiles: jnp.concatenate(tiles, axis=0)
    return (cat(cnt), cat(rank2), cat([jnp.exp(x - v1[0]) for x in t1]),
            cat([jnp.exp(x - v2[0]) * rz for x in t2]), bad)


def _peer_route_kernel(xn2_ref, wqt_ref, sub_ref, cnt_ref, rnk_ref, c1_ref, e2_ref, sc_s):
    tp = xn2_ref.shape[0]
    lanes = 128

    qt = _dot_nt(wqt_ref[...], xn2_ref[...]).astype(BF16)
    for hp in range(2 * PEER_HEADS):
        sc_s[hp] = _dot(sub_ref[hp % 2], qt[hp * PEER_HALF:(hp + 1) * PEER_HALF, :])

    def head_body(h, carry):
        for lg in range(tp // lanes):
            cols = slice(lg * lanes, (lg + 1) * lanes)
            s1 = sc_s[h * 2, :, cols]
            s2 = sc_s[h * 2 + 1, :, cols]

            def store(cnt, rank2, c1, e2):
                cnt_ref[h, :, cols] = cnt
                rnk_ref[h, :, cols] = rank2.astype(BF16)
                c1_ref[h, :, cols] = c1
                e2_ref[h, :, cols] = e2.astype(BF16)

            *tables, bad = _route_fast(s1, s2)
            store(*tables)

            @pl.when(jnp.max(bad) > 0.0)
            def _():
                store(*_route_exact(s1, s2))
        return carry

    lax.fori_loop(0, PEER_HEADS, head_body, 0)


def _peer_route_call(xn2, wqt_b, sub_b):
    n = xn2.shape[0]
    tp = TP_PRE
    tab = pl.BlockSpec((PEER_HEADS, PEER_N_KEYS, tp), lambda i: (0, 0, i))
    return pl.pallas_call(
        _peer_route_kernel,
        grid=(n // tp,),
        in_specs=[pl.BlockSpec((tp, D_MODEL), lambda i: (i, 0)),
                  _const_spec((PEER_HEADS * 2 * PEER_HALF, D_MODEL)), _const_spec((2, PEER_N_KEYS, PEER_HALF))],
        out_specs=[tab] * 4,
        out_shape=[jax.ShapeDtypeStruct((PEER_HEADS, PEER_N_KEYS, n), dt) for dt in (F32, BF16, F32, BF16)],
        scratch_shapes=[pltpu.VMEM((2 * PEER_HEADS, PEER_N_KEYS, tp), F32)],
        compiler_params=pltpu.CompilerParams(dimension_semantics=("arbitrary",), vmem_limit_bytes=VMEM_LIMIT),
        name="peer_route",
    )(xn2, wqt_b, sub_b)


def _peer_dense_kernel(xn2_ref, pu_ref, pvt_ref, cnt_ref, c1_ref, rnk_ref, e2_ref, h_ref, gf_ref,
                       y_ref, acc_ref, at_s, g_s, rnk_s, e2_s, xt_s):
    k = pl.program_id(1)
    eb = at_s.shape[0]
    kps = eb // PEER_N_KEYS
    tm = xn2_ref.shape[0]
    sub_rows = 16
    ntile = PEER_N_KEYS // sub_rows

    @pl.when(k == 0)
    def _():
        acc_ref[...] = jnp.zeros_like(acc_ref)
        rnk_s[...] = rnk_ref[...]
        e2_s[...] = e2_ref[...]
        xt_s[...] = xn2_ref[...].astype(F32).T.astype(BF16)

    def weights_and_gate(row0):
        cnt_rows = [[cnt_ref[h, row0 + ii:row0 + ii + 1, :].astype(BF16) for ii in range(kps)]
                    for h in range(PEER_HEADS)]
        c1_rows = [[c1_ref[h, row0 + ii:row0 + ii + 1, :].astype(BF16) for ii in range(kps)]
                   for h in range(PEER_HEADS)]
        for cg in range(tm // 128):
            cols = slice(cg * 128, (cg + 1) * 128)
            w = [[None] * ntile for _ in range(kps)]
            for h in range(PEER_HEADS):
                cnt_b = [jnp.broadcast_to(cnt_rows[h][ii][:, cols], (sub_rows, 128)) for ii in range(kps)]
                c1_b = [jnp.broadcast_to(c1_rows[h][ii][:, cols], (sub_rows, 128)) for ii in range(kps)]
                for r in range(ntile):
                    jr = slice(r * sub_rows, (r + 1) * sub_rows)
                    rt = rnk_s[h, jr, cols]
                    et = e2_s[h, jr, cols]
                    for ii in range(kps):
                        term = jnp.where(rt < cnt_b[ii], et, 0.0) * c1_b[ii]
                        w[ii][r] = term if w[ii][r] is None else w[ii][r] + term
            for ii in range(kps):
                for r in range(ntile):
                    er = slice(ii * PEER_N_KEYS + r * sub_rows, ii * PEER_N_KEYS + (r + 1) * sub_rows)
                    g_s[er, cols] = _gelu(at_s[er, cols]).astype(BF16) * w[ii][r]

    for b in range(2):
        at_s[...] = _dot(pu_ref[b * eb:(b + 1) * eb, :], xt_s[...])
        weights_and_gate(b * kps)
        acc_ref[...] += _dot(pvt_ref[:, b * eb:(b + 1) * eb], g_s[...])

    @pl.when(k == pl.num_programs(1) - 1)
    def _():
        y_ref[...] = _rms(h_ref[...] + acc_ref[...].T, gf_ref[...])


def _peer_dense_call(xn2, pu_b, pvt_b, cnt, rnk, c1, e2, h, gf):
    n = xn2.shape[0]
    tm, eb = min(TM_DENSE, n), EB_DENSE
    npair = PEER_N_EXPERTS // (2 * eb)
    tok = lambda dt: pl.BlockSpec((tm, D_MODEL), lambda t, k: (t, 0))
    tab = pl.BlockSpec((PEER_HEADS, PEER_N_KEYS, tm), lambda t, k: (0, 0, t))
    rows = pl.BlockSpec((PEER_HEADS, 2 * eb // PEER_N_KEYS, tm), lambda t, k: (0, k, t))
    return pl.pallas_call(
        _peer_dense_kernel,
        grid=(n // tm, npair),
        in_specs=[tok(BF16), pl.BlockSpec((2 * eb, D_MODEL), lambda t, k: (k, 0)),
                  pl.BlockSpec((None, D_MODEL, 2 * eb), lambda t, k: (k, 0, 0)),
                  rows, rows, tab, tab, tok(F32), pl.BlockSpec((1, D_MODEL), lambda t, k: (0, 0))],
        out_specs=tok(F32),
        out_shape=jax.ShapeDtypeStruct((n, D_MODEL), F32),
        scratch_shapes=[pltpu.VMEM((D_MODEL, tm), F32), pltpu.VMEM((eb, tm), F32), pltpu.VMEM((eb, tm), BF16),
                        pltpu.VMEM((PEER_HEADS, PEER_N_KEYS, tm), BF16),
                        pltpu.VMEM((PEER_HEADS, PEER_N_KEYS, tm), BF16),
                        pltpu.VMEM((D_MODEL, tm), BF16)],
        compiler_params=pltpu.CompilerParams(dimension_semantics=("arbitrary", "arbitrary"),
                                             vmem_limit_bytes=VMEM_LIMIT),
        name="peer_dense",
    )(xn2, pu_b, pvt_b, cnt, c1, rnk, e2, h, gf)


def _peer_and_final_norm(h, xn2, wqt_b, sub_b, pu_b, pvt_b, gf):
    cnt, rnk, c1, e2 = _peer_route_call(xn2, wqt_b, sub_b)
    return _peer_dense_call(xn2, pu_b, pvt_b, cnt, rnk, c1, e2, h, gf)


def kernel(x_prompt, x_sample, cache_swa_k, cache_swa_v, cache_mem_k, cache_mem_v, mem_prompt,
           norm1_g, w_in, b_gate, swa_sinks, chunk_norm_g, chunk_ws, chunk_bs, mem_norm_g, w_mem_kv,
           w_branch_out, w_out, norm2_g, peer_wq, peer_subkeys, peer_u, peer_v, final_norm_g):
    depth = norm1_g.shape[0]
    assert depth == 1, "single-layer step"
    batch, seq, _ = x_prompt.shape
    nseq, dec_seq, _ = x_sample.shape

    g1 = norm1_g[0][None]
    g2 = norm2_g[0][None]
    gf = final_norm_g[None]
    cng = chunk_norm_g[0][None]
    w_in_b = w_in[0].astype(BF16)
    wbo_b = w_branch_out[0].astype(BF16)
    wout_b = w_out[0].astype(BF16)
    wqt_b = peer_wq[0].T.astype(BF16)
    sub_b = peer_subkeys[0].astype(BF16)
    pu_b, pvt_b = _expert_tables_call(peer_u, peer_v)
    sinks = swa_sinks[0]
    bst = chunk_bs[0].T
    ws4 = chunk_ws[0][:, :dec_seq, :dec_seq].reshape(-1)
    bs4 = chunk_bs[0][:, :dec_seq].reshape(-1)

    mk, mv, mkb, mvb = _memkv_call(mem_prompt, mem_norm_g[0][None], w_mem_kv[0].astype(BF16))
    hp, xn2p, kp, vp = _prompt_layer_call(
        x_prompt.reshape(batch * seq, D_MODEL), mkb, mvb, sinks, g1, w_in_b, cng, chunk_ws[0], bst,
        b_gate[0], wbo_b, wout_b, g2, batch=batch, seq=seq)
    y_prompt = _peer_and_final_norm(hp, xn2p, wqt_b, sub_b, pu_b, pvt_b, gf).reshape(batch, seq, D_MODEL)

    xs = x_sample.reshape(nseq * dec_seq, D_MODEL)
    qs, *qps, kn, vn, vcs, achk = _sample_proj_call(xs, ws4, bs4, g1, w_in_b, cng, dec_seq=dec_seq)
    aswa, amem, nk, nv = _sample_attn_call(
        sinks, qs, qps, kn, vn,
        cache_swa_k.reshape(nseq, WINDOW, KV_WIDTH), cache_swa_v.reshape(nseq, WINDOW, KV_WIDTH),
        cache_mem_k.reshape(nseq, MEM_LEN * MEM_HEADS, MEM_HEAD_DIM),
        cache_mem_v.reshape(nseq, MEM_LEN * MEM_HEADS, MEM_HEAD_DIM),
        dec_seq=dec_seq)
    hs, xn2s = _sample_merge_call(xs, aswa, achk, amem, g1, w_in_b, b_gate[0], wbo_b, wout_b, g2)
    y_sample = _peer_and_final_norm(hs, xn2s, wqt_b, sub_b, pu_b, pvt_b, gf).reshape(nseq, dec_seq, D_MODEL)

    kv5 = lambda a, b: a.reshape(1, b, WINDOW, SWA_KV_HEADS, SWA_HEAD_DIM)
    mem5 = lambda a: a.reshape(1, batch, MEM_LEN, MEM_HEADS, MEM_HEAD_DIM)
    return (y_prompt, y_sample, kv5(kp, batch), kv5(vp, batch), mem5(mk), mem5(mv),
            kv5(nk, nseq), kv5(nv, nseq), vcs.reshape(1, nseq, dec_seq, CHUNK_GROUPS, CHUNK_GROUP_DIM))
```

```python
import functools

import jax
import jax.numpy as jnp
from jax import lax
from jax.experimental import pallas as pl
from jax.experimental.pallas import tpu as pltpu

F32 = jnp.float32
BF16 = jnp.bfloat16

D_MODEL = 1024
SWA_HEADS = 8
SWA_KV_HEADS = 2
SWA_GROUP = SWA_HEADS // SWA_KV_HEADS
SWA_HEAD_DIM = 64
SWA_WIDTH = SWA_HEADS * SWA_HEAD_DIM
KV_WIDTH = SWA_KV_HEADS * SWA_HEAD_DIM
WINDOW = 128
SWA_SCALE = SWA_HEAD_DIM ** -0.5
CHUNK = 128
CHUNK_GROUPS = 4
CHUNK_GROUP_DIM = 128
CHUNK_WIDTH = CHUNK_GROUPS * CHUNK_GROUP_DIM
MEM_LEN = 256
MEM_HEADS = 4
MEM_HEAD_DIM = 128
MEM_WIDTH = MEM_HEADS * MEM_HEAD_DIM
MEM_SCALE = MEM_HEAD_DIM ** -0.5
N_BRANCH = 3
BRANCH_WIDTH = 512
PEER_HEADS = 8
PEER_N_KEYS = 128
PEER_N_EXPERTS = PEER_N_KEYS * PEER_N_KEYS
PEER_HALF = 128
PEER_TOPK = 16
EPS = 1e-6
NEG = -1e30

O_Q = 0
O_K = O_Q + SWA_WIDTH
O_V = O_K + KV_WIDTH
O_UV = O_V + KV_WIDTH
O_QM = O_UV + 2 * CHUNK_WIDTH
O_GL = O_QM + MEM_WIDTH
IN_WIDTH = O_GL + N_BRANCH * D_MODEL

TM_LAYER = 512
TP_PRE = 512
TM_DENSE = 512
EB_DENSE = 2048
SB_SAMPLE = 8
VMEM_LIMIT = 56 * 1024 * 1024

_NT = (((1,), (1,)), ((), ()))
GELU_HALF = 0.5


def _rms(x, g):
    return x * lax.rsqrt(jnp.mean(x * x, axis=-1, keepdims=True) + EPS) * g


def _gelu(x):
    return 0.5 * x * (1.0 + lax.erf(x * (2.0 ** -0.5)))


def _dot(a, b):
    return jnp.dot(a, b, preferred_element_type=F32)


def _dot_nt(a, b):
    return lax.dot_general(a, b, _NT, preferred_element_type=F32)


def _const_spec(shape):
    nd = len(shape)
    return pl.BlockSpec(shape, lambda *_: (0,) * nd, pipeline_mode=pl.Buffered(1))


def _smem_spec():
    return pl.BlockSpec(memory_space=pltpu.SMEM)


def _chunk_uv(xn, win_ref, cng_ref):
    uv = _gelu(_dot(xn, win_ref[:, O_UV:O_QM]))
    return uv[:, :CHUNK_WIDTH], _rms(uv[:, CHUNK_WIDTH:], cng_ref[...])


def _merge_tail(x, xn, branches, win_ref, bg_ref, wbo_ref, wout_ref, g2_ref, h_ref, xn2_ref):
    acc = None
    for j in range(N_BRANCH):
        pj = _dot(branches[j], wbo_ref[j])
        gl = _dot(xn, win_ref[:, O_GL + j * D_MODEL:O_GL + (j + 1) * D_MODEL]) + bg_ref[j:j + 1, :]
        term = jax.nn.sigmoid(gl) * pj
        acc = term if acc is None else acc + term
    h = x + _dot(acc.astype(BF16), wout_ref[...])
    h_ref[...] = h
    xn2_ref[...] = _rms(h, g2_ref[...]).astype(BF16)


def _softmax_rows(s):
    m = jnp.max(s, axis=-1, keepdims=True)
    e = jnp.exp(s - m)
    return e / jnp.sum(e, axis=-1, keepdims=True)


def _memkv_kernel(mem_ref, g_ref, w_ref, k_ref, v_ref, kb_ref, vb_ref):
    kv = _dot(_rms(mem_ref[...], g_ref[...]).astype(BF16), w_ref[...])
    k = kv[:, :MEM_WIDTH]
    v = kv[:, MEM_WIDTH:]
    k_ref[...] = k
    v_ref[...] = v
    kb_ref[...] = k.astype(BF16)
    vb_ref[...] = v.astype(BF16)


def _memkv_call(mem, g, w_b):
    b, m, d = mem.shape
    blk = lambda w: pl.BlockSpec((None, m, w), lambda i: (i, 0, 0))
    return pl.pallas_call(
        _memkv_kernel,
        grid=(b,),
        in_specs=[blk(d), _const_spec((1, d)), _const_spec((d, 2 * MEM_WIDTH))],
        out_specs=[blk(MEM_WIDTH)] * 4,
        out_shape=[jax.ShapeDtypeStruct((b, m, MEM_WIDTH), F32)] * 2
        + [jax.ShapeDtypeStruct((b, m, MEM_WIDTH), BF16)] * 2,
        compiler_params=pltpu.CompilerParams(dimension_semantics=("arbitrary",), vmem_limit_bytes=VMEM_LIMIT),
        name="mem_kv",
    )(mem, g, w_b)


def _prompt_layer_kernel(sinks_ref, x_ref, mk_ref, mv_ref, g1_ref, win_ref, cng_ref, ws_ref, bst_ref,
                         bg_ref, wbo_ref, wout_ref, g2_ref,
                         h_ref, xn2_ref, kout_ref, vout_ref,
                         kprev_ref, vprev_ref, aswa_ref, achk_ref, amem_ref, *, tiles_per_batch):
    tm = x_ref.shape[0]
    nblk = tm // WINDOW
    i = pl.program_id(0)
    first = (i % tiles_per_batch) == 0

    @pl.when(first)
    def _():
        kprev_ref[...] = jnp.zeros_like(kprev_ref)
        vprev_ref[...] = jnp.zeros_like(vprev_ref)

    x = x_ref[...]
    xn = _rms(x, g1_ref[...]).astype(BF16)

    q = _dot(xn, win_ref[:, O_Q:O_K]).astype(BF16)
    k = _dot(xn, win_ref[:, O_K:O_V])
    v = _dot(xn, win_ref[:, O_V:O_UV])
    kout_ref[...] = k[tm - WINDOW:]
    vout_ref[...] = v[tm - WINDOW:]
    kb = k.astype(BF16)
    vb = v.astype(BF16)
    r = lax.broadcasted_iota(jnp.int32, (WINDOW, 2 * WINDOW), 0)
    c = lax.broadcasted_iota(jnp.int32, (WINDOW, 2 * WINDOW), 1)
    dist = WINDOW + r - c
    band = (dist >= 0) & (dist <= WINDOW)
    distf = dist.astype(F32)
    first_lo = jnp.where(first, WINDOW, 0)
    for n in range(nblk):
        rows = slice(n * WINDOW, (n + 1) * WINDOW)
        if n == 0:
            kp, vp = kprev_ref[...], vprev_ref[...]
            valid = band & (c >= first_lo)
        else:
            prev = slice((n - 1) * WINDOW, n * WINDOW)
            kp, vp = kb[prev], vb[prev]
            valid = band
        kk = jnp.concatenate([kp, kb[rows]], axis=0)
        vv = jnp.concatenate([vp, vb[rows]], axis=0)
        for hq in range(SWA_HEADS):
            kvs = slice((hq // SWA_GROUP) * SWA_HEAD_DIM, (hq // SWA_GROUP + 1) * SWA_HEAD_DIM)
            hs = slice(hq * SWA_HEAD_DIM, (hq + 1) * SWA_HEAD_DIM)
            s = _dot_nt(q[rows, hs], kk[:, kvs]) * SWA_SCALE - (2.0 ** -(hq + 1)) * distf
            s = jnp.where(valid, s, NEG)
            sink = sinks_ref[hq]
            m = jnp.maximum(jnp.max(s, axis=-1, keepdims=True), sink)
            e = jnp.exp(s - m)
            w = e / (jnp.sum(e, axis=-1, keepdims=True) + jnp.exp(sink - m))
            aswa_ref[rows, hs] = _dot(w.astype(BF16), vv[:, kvs]).astype(BF16)
    kprev_ref[...] = kb[tm - WINDOW:]
    vprev_ref[...] = vb[tm - WINDOW:]

    u, vc = _chunk_uv(xn, win_ref, cng_ref)
    vcb = vc.astype(BF16)
    tri = (lax.broadcasted_iota(jnp.int32, (CHUNK, CHUNK), 0)
           >= lax.broadcasted_iota(jnp.int32, (CHUNK, CHUNK), 1))
    for g in range(CHUNK_GROUPS):
        gs = slice(g * CHUNK_GROUP_DIM, (g + 1) * CHUNK_GROUP_DIM)
        wg = jnp.where(tri, ws_ref[g], 0.0).astype(BF16)
        for n in range(nblk):
            rows = slice(n * CHUNK, (n + 1) * CHUNK)
            sg = _dot(wg, vcb[rows, gs]) + bst_ref[:, g:g + 1]
            achk_ref[rows, gs] = (u[rows, gs] * sg).astype(BF16)

    qm = _dot(xn, win_ref[:, O_QM:O_GL]).astype(BF16)
    for hm in range(MEM_HEADS):
        hs = slice(hm * MEM_HEAD_DIM, (hm + 1) * MEM_HEAD_DIM)
        w = _softmax_rows(_dot_nt(qm[:, hs], mk_ref[:, hs]) * MEM_SCALE)
        amem_ref[:, hs] = _dot(w.astype(BF16), mv_ref[:, hs]).astype(BF16)

    _merge_tail(x, xn, (aswa_ref[...], achk_ref[...], amem_ref[...]),
                win_ref, bg_ref, wbo_ref, wout_ref, g2_ref, h_ref, xn2_ref)


def _prompt_layer_call(x2, mkb, mvb, sinks, g1, w_in_b, cng, ws, bst, bg, wbo_b, wout_b, g2, *, batch, seq):
    n = x2.shape[0]
    tm = TM_LAYER
    tpb = seq // tm
    tile = lambda w: pl.BlockSpec((tm, w), lambda i: (i, 0))
    per_batch = lambda r, w: pl.BlockSpec((None, r, w), lambda i: (i // tpb, 0, 0))
    return pl.pallas_call(
        functools.partial(_prompt_layer_kernel, tiles_per_batch=tpb),
        grid=(n // tm,),
        in_specs=[_smem_spec(), tile(D_MODEL), per_batch(MEM_LEN, MEM_WIDTH), per_batch(MEM_LEN, MEM_WIDTH),
                  _const_spec((1, D_MODEL)), _const_spec((D_MODEL, IN_WIDTH)), _const_spec((1, CHUNK_WIDTH)),
                  _const_spec((CHUNK_GROUPS, CHUNK, CHUNK)), _const_spec((CHUNK, CHUNK_GROUPS)),
                  _const_spec((N_BRANCH, D_MODEL)), _const_spec((N_BRANCH, BRANCH_WIDTH, D_MODEL)),
                  _const_spec((D_MODEL, D_MODEL)), _const_spec((1, D_MODEL))],
        out_specs=[tile(D_MODEL), tile(D_MODEL), per_batch(WINDOW, KV_WIDTH), per_batch(WINDOW, KV_WIDTH)],
        out_shape=[jax.ShapeDtypeStruct((n, D_MODEL), F32), jax.ShapeDtypeStruct((n, D_MODEL), BF16),
                   jax.ShapeDtypeStruct((batch, WINDOW, KV_WIDTH), F32),
                   jax.ShapeDtypeStruct((batch, WINDOW, KV_WIDTH), F32)],
        scratch_shapes=[pltpu.VMEM((WINDOW, KV_WIDTH), BF16), pltpu.VMEM((WINDOW, KV_WIDTH), BF16),
                        pltpu.VMEM((tm, BRANCH_WIDTH), BF16), pltpu.VMEM((tm, BRANCH_WIDTH), BF16),
                        pltpu.VMEM((tm, BRANCH_WIDTH), BF16)],
        compiler_params=pltpu.CompilerParams(dimension_semantics=("arbitrary",), vmem_limit_bytes=VMEM_LIMIT),
        name="prompt_layer",
    )(sinks, x2, mkb, mvb, g1, w_in_b, cng, ws, bst, bg, wbo_b, wout_b, g2)


def _sample_proj_kernel(ws4_ref, bs4_ref, x_ref, g1_ref, win_ref, cng_ref,
                        qs_ref, qp0_ref, qp1_ref, qp2_ref, qp3_ref, k_ref, v_ref, vc_ref, achk_ref, qtmp_ref,
                        *, dec_seq):
    n = x_ref.shape[0]
    nseq = n // dec_seq
    xn = _rms(x_ref[...], g1_ref[...]).astype(BF16)
    q = _dot(xn, win_ref[:, O_Q:O_K])
    k_ref[...] = _dot(xn, win_ref[:, O_K:O_V])
    v_ref[...] = _dot(xn, win_ref[:, O_V:O_UV])
    qm = _dot(xn, win_ref[:, O_QM:O_GL])

    lane = lax.broadcasted_iota(jnp.int32, (n, 2 * SWA_HEAD_DIM), 1)
    for hq in range(SWA_HEADS):
        kvh = hq // SWA_GROUP
        win = q[:, (hq // 2) * 128:(hq // 2 + 1) * 128]
        if (hq % 2) != kvh:
            win = pltpu.roll(win, SWA_HEAD_DIM, 1)
        qtmp_ref[...] = jnp.where((lane // SWA_HEAD_DIM) == kvh, win, 0.0)
        for t in range(dec_seq):
            qs_ref[pl.ds(hq * dec_seq + t, nseq, stride=SWA_HEADS * dec_seq), :] = (
                qtmp_ref[pl.ds(t, nseq, stride=dec_seq), :])
    for hm, qp_ref in enumerate((qp0_ref, qp1_ref, qp2_ref, qp3_ref)):
        qp_ref[...] = jnp.zeros_like(qp_ref)
        qtmp_ref[...] = qm[:, hm * MEM_HEAD_DIM:(hm + 1) * MEM_HEAD_DIM]
        for t in range(dec_seq):
            qp_ref[pl.ds(hm * dec_seq + t, nseq, stride=MEM_HEADS * dec_seq), :] = (
                qtmp_ref[pl.ds(t, nseq, stride=dec_seq), :])

    u, vc = _chunk_uv(xn, win_ref, cng_ref)
    vc_ref[...] = vc
    tpos = lax.broadcasted_iota(jnp.int32, (n, CHUNK_GROUP_DIM), 0) % dec_seq
    for g in range(CHUNK_GROUPS):
        gs = slice(g * CHUNK_GROUP_DIM, (g + 1) * CHUNK_GROUP_DIM)
        vg = vc[:, gs]
        sg = jnp.zeros((n, CHUNK_GROUP_DIM), F32)
        for t in range(dec_seq):
            sg = jnp.where(tpos == t, bs4_ref[g * dec_seq + t], sg)
        for d in range(dec_seq):
            shifted = vg if d == 0 else pltpu.roll(vg, d, 0)
            coef = jnp.zeros((n, CHUNK_GROUP_DIM), F32)
            for t in range(d, dec_seq):
                coef = jnp.where(tpos == t, ws4_ref[(g * dec_seq + t) * dec_seq + (t - d)], coef)
            sg = sg + coef * shifted
        achk_ref[:, gs] = (u[:, gs] * sg).astype(BF16)


def _sample_proj_call(x2, ws4, bs4, g1, w_in_b, cng, *, dec_seq):
    n = x2.shape[0]
    nseq = n // dec_seq
    full = lambda r, w: pl.BlockSpec((r, w), lambda i: (0, 0))
    return pl.pallas_call(
        functools.partial(_sample_proj_kernel, dec_seq=dec_seq),
        grid=(1,),
        in_specs=[_smem_spec(), _smem_spec(), full(n, D_MODEL), _const_spec((1, D_MODEL)),
                  _const_spec((D_MODEL, IN_WIDTH)), _const_spec((1, CHUNK_WIDTH))],
        out_specs=[full(nseq * SWA_HEADS * dec_seq, 2 * SWA_HEAD_DIM)]
        + [full(nseq * MEM_HEADS * dec_seq, MEM_HEAD_DIM)] * MEM_HEADS
        + [full(n, KV_WIDTH), full(n, KV_WIDTH), full(n, CHUNK_WIDTH), full(n, CHUNK_WIDTH)],
        out_shape=[jax.ShapeDtypeStruct((nseq * SWA_HEADS * dec_seq, 2 * SWA_HEAD_DIM), F32)]
        + [jax.ShapeDtypeStruct((nseq * MEM_HEADS * dec_seq, MEM_HEAD_DIM), F32)] * MEM_HEADS
        + [jax.ShapeDtypeStruct((n, KV_WIDTH), F32), jax.ShapeDtypeStruct((n, KV_WIDTH), F32),
           jax.ShapeDtypeStruct((n, CHUNK_WIDTH), F32), jax.ShapeDtypeStruct((n, CHUNK_WIDTH), BF16)],
        scratch_shapes=[pltpu.VMEM((n, 128), F32)],
        compiler_params=pltpu.CompilerParams(dimension_semantics=("arbitrary",), vmem_limit_bytes=VMEM_LIMIT),
        name="sample_proj",
    )(ws4, bs4, x2, g1, w_in_b, cng)


def _sample_attn_kernel(sinks_ref, qs_ref, qp0_ref, qp1_ref, qp2_ref, qp3_ref, kn_ref, vn_ref,
                        ck_ref, cv_ref, cmk_ref, cmv_ref,
                        aswa_ref, amem_ref, nk_ref, nv_ref, kk_ref, vv_ref, *, dec_seq):
    sb = ck_ref.shape[0]
    rows_s = SWA_HEADS * dec_seq
    rows_m = MEM_HEADS * dec_seq
    kk_ref[:, WINDOW:, :] = jnp.zeros((sb, WINDOW, KV_WIDTH), F32)
    vv_ref[:, WINDOW:, :] = jnp.zeros((sb, WINDOW, KV_WIDTH), F32)

    m_idx = lax.broadcasted_iota(jnp.int32, (rows_s, 2 * WINDOW), 0)
    c_idx = lax.broadcasted_iota(jnp.int32, (rows_s, 2 * WINDOW), 1)
    dist = WINDOW + (m_idx % dec_seq) - c_idx
    valid = (dist >= 0) & (dist <= WINDOW)
    head = m_idx // dec_seq
    slope_dist = jnp.exp2(-(head + 1).astype(F32)) * dist.astype(F32)
    head_col = lax.broadcasted_iota(jnp.int32, (rows_s, 1), 0) // dec_seq
    sink = jnp.zeros((rows_s, 1), F32)
    for hq in range(SWA_HEADS):
        sink = jnp.where(head_col == hq, sinks_ref[hq], sink)
    lane_kv = lax.broadcasted_iota(jnp.int32, (2 * WINDOW, KV_WIDTH), 1) // SWA_HEAD_DIM
    lane_s = lax.broadcasted_iota(jnp.int32, (dec_seq, SWA_WIDTH), 1) // SWA_HEAD_DIM
    lane_m = lax.broadcasted_iota(jnp.int32, (dec_seq, MEM_WIDTH), 1) // MEM_HEAD_DIM

    for sl in range(sb):
        tok = slice(sl * dec_seq, (sl + 1) * dec_seq)
        kk_ref[sl, 0:WINDOW, :] = ck_ref[sl]
        vv_ref[sl, 0:WINDOW, :] = cv_ref[sl]
        kk_ref[sl, WINDOW:WINDOW + dec_seq, :] = kn_ref[tok, :]
        vv_ref[sl, WINDOW:WINDOW + dec_seq, :] = vn_ref[tok, :]
        nk_ref[sl] = kk_ref[sl, dec_seq:dec_seq + WINDOW, :]
        nv_ref[sl] = vv_ref[sl, dec_seq:dec_seq + WINDOW, :]

        qs = qs_ref[sl * rows_s:(sl + 1) * rows_s, :].astype(BF16)
        s = _dot_nt(qs, kk_ref[sl].astype(BF16)) * SWA_SCALE - slope_dist
        s = jnp.where(valid, s, NEG)
        m = jnp.maximum(jnp.max(s, axis=-1, keepdims=True), sink)
        e = jnp.exp(s - m)
        w = e / (jnp.sum(e, axis=-1, keepdims=True) + jnp.exp(sink - m))
        vv = vv_ref[sl]
        vr = pltpu.roll(vv, SWA_HEAD_DIM, 1)
        dup0 = jnp.where(lane_kv == 0, vv, vr).astype(BF16)
        dup1 = jnp.where(lane_kv == 1, vv, vr).astype(BF16)
        o = _dot(w.astype(BF16), jnp.concatenate([dup0, dup0, dup1, dup1], axis=1))
        acc = jnp.zeros((dec_seq, SWA_WIDTH), F32)
        for hq in range(SWA_HEADS):
            acc = acc + jnp.where(lane_s == hq, o[hq * dec_seq:(hq + 1) * dec_seq, :], 0.0)
        aswa_ref[tok, :] = acc

        qp = jnp.concatenate([r[sl * rows_m:(sl + 1) * rows_m, :] for r in (qp0_ref, qp1_ref, qp2_ref, qp3_ref)],
                             axis=1).astype(BF16)
        mem_k, mem_v = (jnp.concatenate([ref[sl, pl.ds(hm, MEM_LEN, stride=MEM_HEADS), :]
                                         for hm in range(MEM_HEADS)], axis=1).astype(BF16)
                        for ref in (cmk_ref, cmv_ref))
        wm = _softmax_rows(_dot_nt(qp, mem_k) * MEM_SCALE)
        om = _dot(wm.astype(BF16), mem_v)
        accm = jnp.zeros((dec_seq, MEM_WIDTH), F32)
        for hm in range(MEM_HEADS):
            accm = accm + jnp.where(lane_m == hm, om[hm * dec_seq:(hm + 1) * dec_seq, :], 0.0)
        amem_ref[tok, :] = accm


def _sample_attn_call(sinks, qs, qps, kn, vn, ck, cv, cmk, cmv, *, dec_seq):
    nseq = ck.shape[0]
    sb = SB_SAMPLE
    rows = lambda r, w: pl.BlockSpec((sb * r, w), lambda i: (i, 0))
    seqs = lambda r, w: pl.BlockSpec((sb, r, w), lambda i: (i, 0, 0))
    n = nseq * dec_seq
    return pl.pallas_call(
        functools.partial(_sample_attn_kernel, dec_seq=dec_seq),
        grid=(nseq // sb,),
        in_specs=[_smem_spec(), rows(SWA_HEADS * dec_seq, 2 * SWA_HEAD_DIM)]
        + [rows(MEM_HEADS * dec_seq, MEM_HEAD_DIM)] * MEM_HEADS + [
                  rows(dec_seq, KV_WIDTH), rows(dec_seq, KV_WIDTH),
                  seqs(WINDOW, KV_WIDTH), seqs(WINDOW, KV_WIDTH),
                  seqs(MEM_LEN * MEM_HEADS, MEM_HEAD_DIM), seqs(MEM_LEN * MEM_HEADS, MEM_HEAD_DIM)],
        out_specs=[rows(dec_seq, SWA_WIDTH), rows(dec_seq, MEM_WIDTH), seqs(WINDOW, KV_WIDTH), seqs(WINDOW, KV_WIDTH)],
        out_shape=[jax.ShapeDtypeStruct((n, SWA_WIDTH), F32), jax.ShapeDtypeStruct((n, MEM_WIDTH), F32),
                   jax.ShapeDtypeStruct((nseq, WINDOW, KV_WIDTH), F32),
                   jax.ShapeDtypeStruct((nseq, WINDOW, KV_WIDTH), F32)],
        scratch_shapes=[pltpu.VMEM((sb, 2 * WINDOW, KV_WIDTH), F32), pltpu.VMEM((sb, 2 * WINDOW, KV_WIDTH), F32)],
        compiler_params=pltpu.CompilerParams(dimension_semantics=("arbitrary",), vmem_limit_bytes=VMEM_LIMIT),
        name="sample_attn",
    )(sinks, qs, *qps, kn, vn, ck, cv, cmk, cmv)


def _sample_merge_kernel(x_ref, aswa_ref, achk_ref, amem_ref, g1_ref, win_ref, bg_ref, wbo_ref, wout_ref, g2_ref,
                         h_ref, xn2_ref):
    x = x_ref[...]
    xn = _rms(x, g1_ref[...]).astype(BF16)
    _merge_tail(x, xn, (aswa_ref[...].astype(BF16), achk_ref[...], amem_ref[...].astype(BF16)),
                win_ref, bg_ref, wbo_ref, wout_ref, g2_ref, h_ref, xn2_ref)


def _sample_merge_call(x2, aswa, achk, amem, g1, w_in_b, bg, wbo_b, wout_b, g2):
    n = x2.shape[0]
    full = lambda w: pl.BlockSpec((n, w), lambda i: (0, 0))
    return pl.pallas_call(
        _sample_merge_kernel,
        grid=(1,),
        in_specs=[full(D_MODEL), full(BRANCH_WIDTH), full(BRANCH_WIDTH), full(BRANCH_WIDTH),
                  _const_spec((1, D_MODEL)), _const_spec((D_MODEL, IN_WIDTH)), _const_spec((N_BRANCH, D_MODEL)),
                  _const_spec((N_BRANCH, BRANCH_WIDTH, D_MODEL)), _const_spec((D_MODEL, D_MODEL)),
                  _const_spec((1, D_MODEL))],
        out_specs=[full(D_MODEL), full(D_MODEL)],
        out_shape=[jax.ShapeDtypeStruct((n, D_MODEL), F32), jax.ShapeDtypeStruct((n, D_MODEL), BF16)],
        compiler_params=pltpu.CompilerParams(dimension_semantics=("arbitrary",), vmem_limit_bytes=VMEM_LIMIT),
        name="sample_merge",
    )(x2, aswa, achk, amem, g1, w_in_b, bg, wbo_b, wout_b, g2)


def _expert_tables_kernel(pu_ref, pv_ref, pub_ref, pvt_ref):
    pub_ref[...] = pu_ref[...].astype(BF16)
    pvt_ref[...] = pv_ref[...].T.astype(BF16)


def _expert_tables_call(peer_u, peer_v):
    _, ne, d = peer_u.shape
    rows = EB_DENSE
    blk = pl.BlockSpec((None, rows, d), lambda i: (0, i, 0))
    return pl.pallas_call(
        _expert_tables_kernel,
        grid=(ne // rows,),
        in_specs=[blk, blk],
        out_specs=[pl.BlockSpec((rows, d), lambda i: (i, 0)), pl.BlockSpec((None, d, rows), lambda i: (i, 0, 0))],
        out_shape=[jax.ShapeDtypeStruct((ne, d), BF16), jax.ShapeDtypeStruct((ne // rows, d, rows), BF16)],
        compiler_params=pltpu.CompilerParams(dimension_semantics=("arbitrary",), vmem_limit_bytes=VMEM_LIMIT),
        name="expert_tables",
    )(peer_u, peer_v)


def _top16(s, key_iota):
    rank = jnp.full(s.shape, float(PEER_TOPK), F32)
    vals = []
    for kth in range(PEER_TOPK):
        m = jnp.max(s, axis=0, keepdims=True)
        idx = jnp.min(jnp.where(s == m, key_iota, float(PEER_N_KEYS)), axis=0, keepdims=True)
        hit = key_iota == idx
        rank = jnp.where(hit, float(kth), rank)
        s = jnp.where(hit, -jnp.inf, s)
        vals.append(m)
    return rank, vals


def _route_exact(s1, s2):
    lanes = s1.shape[1]
    key_iota = lax.broadcasted_iota(jnp.int32, (PEER_N_KEYS, lanes), 0).astype(F32)
    cand_iota = lax.broadcasted_iota(jnp.int32, (PEER_TOPK * PEER_TOPK, lanes), 0).astype(F32)
    k16 = lax.broadcasted_iota(jnp.int32, (PEER_TOPK, lanes), 0)
    rank1, v1 = _top16(s1, key_iota)
    rank2, v2 = _top16(s2, key_iota)
    top2 = jnp.zeros((PEER_TOPK, lanes), F32)
    for b in range(PEER_TOPK):
        top2 = jnp.where(k16 == b, v2[b], top2)
    cand = jnp.concatenate([v1[a] + top2 for a in range(PEER_TOPK)], axis=0)
    sel = jnp.zeros(cand.shape, F32)
    cmax = v1[0] + v2[0]
    z = jnp.zeros((1, lanes), F32)
    for _ in range(PEER_TOPK):
        m = jnp.max(cand, axis=0, keepdims=True)
        idx = jnp.min(jnp.where(cand == m, cand_iota, float(PEER_TOPK * PEER_TOPK)), axis=0, keepdims=True)
        hit = cand_iota == idx
        sel = jnp.where(hit, 1.0, sel)
        cand = jnp.where(hit, -jnp.inf, cand)
        z = z + jnp.exp(m - cmax)
    cnt = jnp.zeros((PEER_N_KEYS, lanes), F32)
    for a in range(PEER_TOPK):
        n_a = jnp.sum(sel[a * PEER_TOPK:(a + 1) * PEER_TOPK], axis=0, keepdims=True)
        cnt = jnp.where(rank1 == float(a), n_a, cnt)
    return cnt, rank2, jnp.exp(s1 - v1[0]), jnp.exp(s2 - v2[0]) * (GELU_HALF / z)


def _sort16_desc(v):
    n = len(v)
    k = 2
    while k <= n:
        j = k // 2
        while j >= 1:
            for i in range(n):
                l = i ^ j
                if l > i:
                    hi, lo = jnp.maximum(v[i], v[l]), jnp.minimum(v[i], v[l])
                    v[i], v[l] = (hi, lo) if (i & k) == 0 else (lo, hi)
            j //= 2
        k *= 2


def _merge16_desc(v):
    j = len(v) // 2
    while j >= 1:
        for i in range(len(v)):
            l = i ^ j
            if l > i:
                v[i], v[l] = jnp.maximum(v[i], v[l]), jnp.minimum(v[i], v[l])
        j //= 2


def _sublane_allsum(x):
    for shift in (4, 2, 1):
        x = x + pltpu.roll(x, shift, 0)
    return x


def _sorted_top16(tiles):
    v = list(tiles)
    _sort16_desc(v)
    for shift in (4, 2, 1):
        v = [jnp.maximum(v[r], pltpu.roll(v[PEER_TOPK - 1 - r], shift, 0)) for r in range(PEER_TOPK)]
        _merge16_desc(v)
    return v


def _route_fast(s1, s2):
    nt = PEER_N_KEYS // 8
    lanes = s1.shape[1]
    sub = lax.broadcasted_iota(jnp.int32, (8, lanes), 0)
    ninf = jnp.full((8, lanes), -jnp.inf, F32)
    zero = jnp.zeros((8, lanes), F32)
    t1 = [s1[r * 8:(r + 1) * 8, :] for r in range(nt)]
    t2 = [s2[r * 8:(r + 1) * 8, :] for r in range(nt)]
    v1 = _sorted_top16(t1)
    v2 = _sorted_top16(t2)

    rank2 = []
    for x in t2:
        r = jnp.full((8, lanes), float(PEER_TOPK), F32)
        for k in range(PEER_TOPK - 1, -1, -1):
            r = jnp.where(x >= v2[k], float(k), r)
        rank2.append(r)

    def rows(v, base):
        out = v[base]
        for b in range(1, 8):
            out = jnp.where(sub == b, v[base + b], out)
        return out

    r1 = [rows(v1, 0), rows(v1, 8)]
    r2 = [rows(v2, 0), rows(v2, 8)]
    cands = [v1[0] + r2[0], v1[0] + r2[1], v1[1] + r2[0],
             jnp.where(sub < 5, v1[2] + r2[0], ninf), jnp.where(sub < 4, v1[3] + r2[0], ninf),
             jnp.where(sub >= 4, v2[0] + r1[0], ninf), v2[0] + r1[1],
             jnp.where(sub >= 4, v2[1] + r1[0], ninf), jnp.where(sub == 4, v2[2] + r1[0], ninf)]
    top = _sorted_top16(cands + [ninf] * (PEER_TOPK - len(cands)))
    tau, cmax = top[PEER_TOPK - 1], top[0]
    z = zero
    for c in cands:
        z = z + jnp.where(c >= tau, jnp.exp(c - cmax), 0.0)
    z = _sublane_allsum(z)

    cnt = []
    for x in t1:
        c = zero
        for b in range(3):
            c = jnp.where(x + v2[b] >= tau, float(b + 1), c)
        cnt.append(c)
    for a in range(4):
        n_a = zero
        for b in range(PEER_TOPK // (a + 1)):
            n_a = jnp.where(v1[a] + v2[b] >= tau, float(b + 1), n_a)
        cnt = [jnp.where(x == v1[a], n_a, c) for x, c in zip(t1, cnt)]

    total = _sublane_allsum(functools.reduce(lambda a, b: a + b, cnt))
    ranked = _sublane_allsum(functools.reduce(lambda a, b: a + b,
                                              [jnp.where(r < float(PEER_TOPK), 1.0, 0.0) for r in rank2]))
    bad = jnp.where(total != float(PEER_TOPK), 1.0, 0.0) + jnp.where(ranked != float(PEER_TOPK), 1.0, 0.0)
    for k in range(PEER_TOPK - 1):
        bad = bad + jnp.where(v2[k] == v2[k + 1], 1.0, 0.0)
    for k in range(4):
        bad = bad + jnp.where(v1[k] == v1[k + 1], 1.0, 0.0)

    rz = GELU_HALF / z
    cat = lambda tiles: jnp.concatenate(tiles, axis=0)
    return (cat(cnt), cat(rank2), cat([jnp.exp(x - v1[0]) for x in t1]),
            cat([jnp.exp(x - v2[0]) * rz for x in t2]), bad)


def _peer_route_kernel(xn2_ref, wqt_ref, sub_ref, cnt_ref, rnk_ref, c1_ref, e2_ref, sc_s):
    tp = xn2_ref.shape[0]
    lanes = 128

    qt = _dot_nt(wqt_ref[...], xn2_ref[...]).astype(BF16)
    for hp in range(2 * PEER_HEADS):
        sc_s[hp] = _dot(sub_ref[hp % 2], qt[hp * PEER_HALF:(hp + 1) * PEER_HALF, :])

    def head_body(h, carry):
        for lg in range(tp // lanes):
            cols = slice(lg * lanes, (lg + 1) * lanes)
            s1 = sc_s[h * 2, :, cols]
            s2 = sc_s[h * 2 + 1, :, cols]

            def store(cnt, rank2, c1, e2):
                cnt_ref[h, :, cols] = cnt
                rnk_ref[h, :, cols] = rank2.astype(BF16)
                c1_ref[h, :, cols] = c1
                e2_ref[h, :, cols] = e2.astype(BF16)

            *tables, bad = _route_fast(s1, s2)
            store(*tables)

            @pl.when(jnp.max(bad) > 0.0)
            def _():
                store(*_route_exact(s1, s2))
        return carry

    lax.fori_loop(0, PEER_HEADS, head_body, 0)


def _peer_route_call(xn2, wqt_b, sub_b):
    n = xn2.shape[0]
    tp = TP_PRE
    tab = pl.BlockSpec((PEER_HEADS, PEER_N_KEYS, tp), lambda i: (0, 0, i))
    return pl.pallas_call(
        _peer_route_kernel,
        grid=(n // tp,),
        in_specs=[pl.BlockSpec((tp, D_MODEL), lambda i: (i, 0)),
                  _const_spec((PEER_HEADS * 2 * PEER_HALF, D_MODEL)), _const_spec((2, PEER_N_KEYS, PEER_HALF))],
        out_specs=[tab] * 4,
        out_shape=[jax.ShapeDtypeStruct((PEER_HEADS, PEER_N_KEYS, n), dt) for dt in (F32, BF16, F32, BF16)],
        scratch_shapes=[pltpu.VMEM((2 * PEER_HEADS, PEER_N_KEYS, tp), F32)],
        compiler_params=pltpu.CompilerParams(dimension_semantics=("arbitrary",), vmem_limit_bytes=VMEM_LIMIT),
        name="peer_route",
    )(xn2, wqt_b, sub_b)


def _peer_dense_kernel(xn2_ref, pu_ref, pvt_ref, cnt_ref, c1_ref, rnk_ref, e2_ref, h_ref, gf_ref,
                       y_ref, acc_ref, at_s, g_s, rnk_s, e2_s, xt_s):
    k = pl.program_id(1)
    kps = 4
    eb = kps * PEER_N_KEYS
    tm = xn2_ref.shape[0]
    sub_rows = 16
    ntile = PEER_N_KEYS // sub_rows

    @pl.when(k == 0)
    def _():
        acc_ref[...] = jnp.zeros_like(acc_ref)
        rnk_s[...] = rnk_ref[...]
        e2_s[...] = e2_ref[...]
        xt_s[...] = xn2_ref[...].astype(F32).T.astype(BF16)

    def weights_and_gate(row0, base):
        cnt_rows = [[cnt_ref[h, row0 + ii:row0 + ii + 1, :].astype(BF16) for ii in range(kps)]
                    for h in range(PEER_HEADS)]
        c1_rows = [[c1_ref[h, row0 + ii:row0 + ii + 1, :].astype(BF16) for ii in range(kps)]
                   for h in range(PEER_HEADS)]
        for cg in range(tm // 128):
            cols = slice(cg * 128, (cg + 1) * 128)
            w = [[None] * ntile for _ in range(kps)]
            for h in range(PEER_HEADS):
                cnt_b = [jnp.broadcast_to(cnt_rows[h][ii][:, cols], (sub_rows, 128)) for ii in range(kps)]
                c1_b = [jnp.broadcast_to(c1_rows[h][ii][:, cols], (sub_rows, 128)) for ii in range(kps)]
                for r in range(ntile):
                    jr = slice(r * sub_rows, (r + 1) * sub_rows)
                    rt = rnk_s[h, jr, cols]
                    et = e2_s[h, jr, cols]
                    for ii in range(kps):
                        term = jnp.where(rt < cnt_b[ii], et, 0.0) * c1_b[ii]
                        w[ii][r] = term if w[ii][r] is None else w[ii][r] + term
            for ii in range(kps):
                for r in range(ntile):
                    e0 = base + ii * PEER_N_KEYS + r * sub_rows
                    er = slice(e0, e0 + sub_rows)
                    a = at_s[er, cols]
                    g_s[er, cols] = (a * (1.0 + lax.erf(a * (2.0 ** -0.5)))).astype(BF16) * w[ii][r]

    at_s[...] = _dot(pu_ref[...], xt_s[...])
    for b in range(at_s.shape[0] // eb):
        weights_and_gate(b * kps, b * eb)
    acc_ref[...] += _dot(pvt_ref[...], g_s[...])

    @pl.when(k == pl.num_programs(1) - 1)
    def _():
        y_ref[...] = _rms(h_ref[...] + acc_ref[...].T, gf_ref[...])


def _peer_dense_call(xn2, pu_b, pvt_b, cnt, rnk, c1, e2, h, gf):
    n = xn2.shape[0]
    tm, eb = min(TM_DENSE, n), EB_DENSE
    nstep = PEER_N_EXPERTS // eb
    tok = lambda dt: pl.BlockSpec((tm, D_MODEL), lambda t, k: (t, 0))
    tab = pl.BlockSpec((PEER_HEADS, PEER_N_KEYS, tm), lambda t, k: (0, 0, t))
    rows = pl.BlockSpec((PEER_HEADS, eb // PEER_N_KEYS, tm), lambda t, k: (0, k, t))
    return pl.pallas_call(
        _peer_dense_kernel,
        grid=(n // tm, nstep),
        in_specs=[tok(BF16), pl.BlockSpec((eb, D_MODEL), lambda t, k: (k, 0)),
                  pl.BlockSpec((None, D_MODEL, eb), lambda t, k: (k, 0, 0)),
                  rows, rows, tab, tab, tok(F32), pl.BlockSpec((1, D_MODEL), lambda t, k: (0, 0))],
        out_specs=tok(F32),
        out_shape=jax.ShapeDtypeStruct((n, D_MODEL), F32),
        scratch_shapes=[pltpu.VMEM((D_MODEL, tm), F32), pltpu.VMEM((eb, tm), F32), pltpu.VMEM((eb, tm), BF16),
                        pltpu.VMEM((PEER_HEADS, PEER_N_KEYS, tm), BF16),
                        pltpu.VMEM((PEER_HEADS, PEER_N_KEYS, tm), BF16),
                        pltpu.VMEM((D_MODEL, tm), BF16)],
        compiler_params=pltpu.CompilerParams(dimension_semantics=("arbitrary", "arbitrary"),
                                             vmem_limit_bytes=VMEM_LIMIT),
        name="peer_dense",
    )(xn2, pu_b, pvt_b, cnt, c1, rnk, e2, h, gf)


def _peer_and_final_norm(h, xn2, wqt_b, sub_b, pu_b, pvt_b, gf):
    cnt, rnk, c1, e2 = _peer_route_call(xn2, wqt_b, sub_b)
    return _peer_dense_call(xn2, pu_b, pvt_b, cnt, rnk, c1, e2, h, gf)


def kernel(x_prompt, x_sample, cache_swa_k, cache_swa_v, cache_mem_k, cache_mem_v, mem_prompt,
           norm1_g, w_in, b_gate, swa_sinks, chunk_norm_g, chunk_ws, chunk_bs, mem_norm_g, w_mem_kv,
           w_branch_out, w_out, norm2_g, peer_wq, peer_subkeys, peer_u, peer_v, final_norm_g):
    depth = norm1_g.shape[0]
    assert depth == 1, "single-layer step"
    batch, seq, _ = x_prompt.shape
    nseq, dec_seq, _ = x_sample.shape

    g1 = norm1_g[0][None]
    g2 = norm2_g[0][None]
    gf = final_norm_g[None]
    cng = chunk_norm_g[0][None]
    w_in_b = w_in[0].astype(BF16)
    wbo_b = w_branch_out[0].astype(BF16)
    wout_b = w_out[0].astype(BF16)
    wqt_b = peer_wq[0].T.astype(BF16)
    sub_b = peer_subkeys[0].astype(BF16)
    pu_b, pvt_b = _expert_tables_call(peer_u, peer_v)
    sinks = swa_sinks[0]
    bst = chunk_bs[0].T
    ws4 = chunk_ws[0][:, :dec_seq, :dec_seq].reshape(-1)
    bs4 = chunk_bs[0][:, :dec_seq].reshape(-1)

    mk, mv, mkb, mvb = _memkv_call(mem_prompt, mem_norm_g[0][None], w_mem_kv[0].astype(BF16))
    hp, xn2p, kp, vp = _prompt_layer_call(
        x_prompt.reshape(batch * seq, D_MODEL), mkb, mvb, sinks, g1, w_in_b, cng, chunk_ws[0], bst,
        b_gate[0], wbo_b, wout_b, g2, batch=batch, seq=seq)
    y_prompt = _peer_and_final_norm(hp, xn2p, wqt_b, sub_b, pu_b, pvt_b, gf).reshape(batch, seq, D_MODEL)

    xs = x_sample.reshape(nseq * dec_seq, D_MODEL)
    qs, *qps, kn, vn, vcs, achk = _sample_proj_call(xs, ws4, bs4, g1, w_in_b, cng, dec_seq=dec_seq)
    aswa, amem, nk, nv = _sample_attn_call(
        sinks, qs, qps, kn, vn,
        cache_swa_k.reshape(nseq, WINDOW, KV_WIDTH), cache_swa_v.reshape(nseq, WINDOW, KV_WIDTH),
        cache_mem_k.reshape(nseq, MEM_LEN * MEM_HEADS, MEM_HEAD_DIM),
        cache_mem_v.reshape(nseq, MEM_LEN * MEM_HEADS, MEM_HEAD_DIM),
        dec_seq=dec_seq)
    hs, xn2s = _sample_merge_call(xs, aswa, achk, amem, g1, w_in_b, b_gate[0], wbo_b, wout_b, g2)
    y_sample = _peer_and_final_norm(hs, xn2s, wqt_b, sub_b, pu_b, pvt_b, gf).reshape(nseq, dec_seq, D_MODEL)

    kv5 = lambda a, b: a.reshape(1, b, WINDOW, SWA_KV_HEADS, SWA_HEAD_DIM)
    mem5 = lambda a: a.reshape(1, batch, MEM_LEN, MEM_HEADS, MEM_HEAD_DIM)
    return (y_prompt, y_sample, kv5(kp, batch), kv5(vp, batch), mem5(mk), mem5(mv),
            kv5(nk, nseq), kv5(nv, nseq), vcs.reshape(1, nseq, dec_seq, CHUNK_GROUPS, CHUNK_GROUP_DIM))
```

```python
import functools

import jax
import jax.numpy as jnp
from jax import lax
from jax.experimental import pallas as pl
from jax.experimental.pallas import tpu as pltpu

F32 = jnp.float32
BF16 = jnp.bfloat16

D_MODEL = 1024
SWA_HEADS = 8
SWA_KV_HEADS = 2
SWA_GROUP = SWA_HEADS // SWA_KV_HEADS
SWA_HEAD_DIM = 64
SWA_WIDTH = SWA_HEADS * SWA_HEAD_DIM
KV_WIDTH = SWA_KV_HEADS * SWA_HEAD_DIM
WINDOW = 128
SWA_SCALE = SWA_HEAD_DIM ** -0.5
CHUNK = 128
CHUNK_GROUPS = 4
CHUNK_GROUP_DIM = 128
CHUNK_WIDTH = CHUNK_GROUPS * CHUNK_GROUP_DIM
MEM_LEN = 256
MEM_HEADS = 4
MEM_HEAD_DIM = 128
MEM_WIDTH = MEM_HEADS * MEM_HEAD_DIM
MEM_SCALE = MEM_HEAD_DIM ** -0.5
N_BRANCH = 3
BRANCH_WIDTH = 512
PEER_HEADS = 8
PEER_N_KEYS = 128
PEER_N_EXPERTS = PEER_N_KEYS * PEER_N_KEYS
PEER_HALF = 128
PEER_TOPK = 16
EPS = 1e-6
NEG = -1e30

O_Q = 0
O_K = O_Q + SWA_WIDTH
O_V = O_K + KV_WIDTH
O_UV = O_V + KV_WIDTH
O_QM = O_UV + 2 * CHUNK_WIDTH
O_GL = O_QM + MEM_WIDTH
IN_WIDTH = O_GL + N_BRANCH * D_MODEL

TM_LAYER = 512
TP_PRE = 512
TM_DENSE = 512
EB_DENSE = 2048
SB_SAMPLE = 8
VMEM_LIMIT = 56 * 1024 * 1024

_NT = (((1,), (1,)), ((), ()))
GELU_HALF = 0.5


def _rms(x, g):
    return x * lax.rsqrt(jnp.mean(x * x, axis=-1, keepdims=True) + EPS) * g


def _gelu(x):
    return 0.5 * x * (1.0 + lax.erf(x * (2.0 ** -0.5)))


def _dot(a, b):
    return jnp.dot(a, b, preferred_element_type=F32)


def _dot_nt(a, b):
    return lax.dot_general(a, b, _NT, preferred_element_type=F32)


def _const_spec(shape):
    nd = len(shape)
    return pl.BlockSpec(shape, lambda *_: (0,) * nd, pipeline_mode=pl.Buffered(1))


def _smem_spec():
    return pl.BlockSpec(memory_space=pltpu.SMEM)


def _chunk_uv(xn, win_ref, cng_ref):
    uv = _gelu(_dot(xn, win_ref[:, O_UV:O_QM]))
    return uv[:, :CHUNK_WIDTH], _rms(uv[:, CHUNK_WIDTH:], cng_ref[...])


def _merge_tail(x, xn, branches, win_ref, bg_ref, wbo_ref, wout_ref, g2_ref, h_ref, xn2_ref):
    acc = None
    for j in range(N_BRANCH):
        pj = _dot(branches[j], wbo_ref[j])
        gl = _dot(xn, win_ref[:, O_GL + j * D_MODEL:O_GL + (j + 1) * D_MODEL]) + bg_ref[j:j + 1, :]
        term = jax.nn.sigmoid(gl) * pj
        acc = term if acc is None else acc + term
    h = x + _dot(acc.astype(BF16), wout_ref[...])
    h_ref[...] = h
    xn2_ref[...] = _rms(h, g2_ref[...]).astype(BF16)


def _softmax_rows(s):
    m = jnp.max(s, axis=-1, keepdims=True)
    e = jnp.exp(s - m)
    return e / jnp.sum(e, axis=-1, keepdims=True)


def _memkv_kernel(mem_ref, g_ref, w_ref, k_ref, v_ref, kb_ref, vb_ref):
    kv = _dot(_rms(mem_ref[...], g_ref[...]).astype(BF16), w_ref[...])
    k = kv[:, :MEM_WIDTH]
    v = kv[:, MEM_WIDTH:]
    k_ref[...] = k
    v_ref[...] = v
    kb_ref[...] = k.astype(BF16)
    vb_ref[...] = v.astype(BF16)


def _memkv_call(mem, g, w_b):
    b, m, d = mem.shape
    blk = lambda w: pl.BlockSpec((None, m, w), lambda i: (i, 0, 0))
    return pl.pallas_call(
        _memkv_kernel,
        grid=(b,),
        in_specs=[blk(d), _const_spec((1, d)), _const_spec((d, 2 * MEM_WIDTH))],
        out_specs=[blk(MEM_WIDTH)] * 4,
        out_shape=[jax.ShapeDtypeStruct((b, m, MEM_WIDTH), F32)] * 2
        + [jax.ShapeDtypeStruct((b, m, MEM_WIDTH), BF16)] * 2,
        compiler_params=pltpu.CompilerParams(dimension_semantics=("arbitrary",), vmem_limit_bytes=VMEM_LIMIT),
        name="mem_kv",
    )(mem, g, w_b)


def _prompt_layer_kernel(sinks_ref, x_ref, mk_ref, mv_ref, g1_ref, win_ref, cng_ref, ws_ref, bst_ref,
                         bg_ref, wbo_ref, wout_ref, g2_ref,
                         h_ref, xn2_ref, kout_ref, vout_ref,
                         kprev_ref, vprev_ref, aswa_ref, achk_ref, amem_ref, *, tiles_per_batch):
    tm = x_ref.shape[0]
    nblk = tm // WINDOW
    i = pl.program_id(0)
    first = (i % tiles_per_batch) == 0

    @pl.when(first)
    def _():
        kprev_ref[...] = jnp.zeros_like(kprev_ref)
        vprev_ref[...] = jnp.zeros_like(vprev_ref)

    x = x_ref[...]
    xn = _rms(x, g1_ref[...]).astype(BF16)

    q = _dot(xn, win_ref[:, O_Q:O_K]).astype(BF16)
    k = _dot(xn, win_ref[:, O_K:O_V])
    v = _dot(xn, win_ref[:, O_V:O_UV])
    kout_ref[...] = k[tm - WINDOW:]
    vout_ref[...] = v[tm - WINDOW:]
    kb = k.astype(BF16)
    vb = v.astype(BF16)
    r = lax.broadcasted_iota(jnp.int32, (WINDOW, 2 * WINDOW), 0)
    c = lax.broadcasted_iota(jnp.int32, (WINDOW, 2 * WINDOW), 1)
    dist = WINDOW + r - c
    band = (dist >= 0) & (dist <= WINDOW)
    distf = dist.astype(F32)
    first_lo = jnp.where(first, WINDOW, 0)
    for n in range(nblk):
        rows = slice(n * WINDOW, (n + 1) * WINDOW)
        if n == 0:
            kp, vp = kprev_ref[...], vprev_ref[...]
            valid = band & (c >= first_lo)
        else:
            prev = slice((n - 1) * WINDOW, n * WINDOW)
            kp, vp = kb[prev], vb[prev]
            valid = band
        kk = jnp.concatenate([kp, kb[rows]], axis=0)
        vv = jnp.concatenate([vp, vb[rows]], axis=0)
        for hq in range(SWA_HEADS):
            kvs = slice((hq // SWA_GROUP) * SWA_HEAD_DIM, (hq // SWA_GROUP + 1) * SWA_HEAD_DIM)
            hs = slice(hq * SWA_HEAD_DIM, (hq + 1) * SWA_HEAD_DIM)
            s = _dot_nt(q[rows, hs], kk[:, kvs]) * SWA_SCALE - (2.0 ** -(hq + 1)) * distf
            s = jnp.where(valid, s, NEG)
            sink = sinks_ref[hq]
            m = jnp.maximum(jnp.max(s, axis=-1, keepdims=True), sink)
            e = jnp.exp(s - m)
            w = e / (jnp.sum(e, axis=-1, keepdims=True) + jnp.exp(sink - m))
            aswa_ref[rows, hs] = _dot(w.astype(BF16), vv[:, kvs]).astype(BF16)
    kprev_ref[...] = kb[tm - WINDOW:]
    vprev_ref[...] = vb[tm - WINDOW:]

    u, vc = _chunk_uv(xn, win_ref, cng_ref)
    vcb = vc.astype(BF16)
    tri = (lax.broadcasted_iota(jnp.int32, (CHUNK, CHUNK), 0)
           >= lax.broadcasted_iota(jnp.int32, (CHUNK, CHUNK), 1))
    for g in range(CHUNK_GROUPS):
        gs = slice(g * CHUNK_GROUP_DIM, (g + 1) * CHUNK_GROUP_DIM)
        wg = jnp.where(tri, ws_ref[g], 0.0).astype(BF16)
        for n in range(nblk):
            rows = slice(n * CHUNK, (n + 1) * CHUNK)
            sg = _dot(wg, vcb[rows, gs]) + bst_ref[:, g:g + 1]
            achk_ref[rows, gs] = (u[rows, gs] * sg).astype(BF16)

    qm = _dot(xn, win_ref[:, O_QM:O_GL]).astype(BF16)
    for hm in range(MEM_HEADS):
        hs = slice(hm * MEM_HEAD_DIM, (hm + 1) * MEM_HEAD_DIM)
        w = _softmax_rows(_dot_nt(qm[:, hs], mk_ref[:, hs]) * MEM_SCALE)
        amem_ref[:, hs] = _dot(w.astype(BF16), mv_ref[:, hs]).astype(BF16)

    _merge_tail(x, xn, (aswa_ref[...], achk_ref[...], amem_ref[...]),
                win_ref, bg_ref, wbo_ref, wout_ref, g2_ref, h_ref, xn2_ref)


def _prompt_layer_call(x2, mkb, mvb, sinks, g1, w_in_b, cng, ws, bst, bg, wbo_b, wout_b, g2, *, batch, seq):
    n = x2.shape[0]
    tm = TM_LAYER
    tpb = seq // tm
    tile = lambda w: pl.BlockSpec((tm, w), lambda i: (i, 0))
    per_batch = lambda r, w: pl.BlockSpec((None, r, w), lambda i: (i // tpb, 0, 0))
    return pl.pallas_call(
        functools.partial(_prompt_layer_kernel, tiles_per_batch=tpb),
        grid=(n // tm,),
        in_specs=[_smem_spec(), tile(D_MODEL), per_batch(MEM_LEN, MEM_WIDTH), per_batch(MEM_LEN, MEM_WIDTH),
                  _const_spec((1, D_MODEL)), _const_spec((D_MODEL, IN_WIDTH)), _const_spec((1, CHUNK_WIDTH)),
                  _const_spec((CHUNK_GROUPS, CHUNK, CHUNK)), _const_spec((CHUNK, CHUNK_GROUPS)),
                  _const_spec((N_BRANCH, D_MODEL)), _const_spec((N_BRANCH, BRANCH_WIDTH, D_MODEL)),
                  _const_spec((D_MODEL, D_MODEL)), _const_spec((1, D_MODEL))],
        out_specs=[tile(D_MODEL), tile(D_MODEL), per_batch(WINDOW, KV_WIDTH), per_batch(WINDOW, KV_WIDTH)],
        out_shape=[jax.ShapeDtypeStruct((n, D_MODEL), F32), jax.ShapeDtypeStruct((n, D_MODEL), BF16),
                   jax.ShapeDtypeStruct((batch, WINDOW, KV_WIDTH), F32),
                   jax.ShapeDtypeStruct((batch, WINDOW, KV_WIDTH), F32)],
        scratch_shapes=[pltpu.VMEM((WINDOW, KV_WIDTH), BF16), pltpu.VMEM((WINDOW, KV_WIDTH), BF16),
                        pltpu.VMEM((tm, BRANCH_WIDTH), BF16), pltpu.VMEM((tm, BRANCH_WIDTH), BF16),
                        pltpu.VMEM((tm, BRANCH_WIDTH), BF16)],
        compiler_params=pltpu.CompilerParams(dimension_semantics=("arbitrary",), vmem_limit_bytes=VMEM_LIMIT),
        name="prompt_layer",
    )(sinks, x2, mkb, mvb, g1, w_in_b, cng, ws, bst, bg, wbo_b, wout_b, g2)


def _sample_proj_kernel(ws4_ref, bs4_ref, x_ref, g1_ref, win_ref, cng_ref,
                        qs_ref, qp_ref, k_ref, v_ref, vc_ref, achk_ref, *, dec_seq):
    n = x_ref.shape[0]
    xn = _rms(x_ref[...], g1_ref[...]).astype(BF16)
    q = _dot(xn, win_ref[:, O_Q:O_K])
    k_ref[...] = _dot(xn, win_ref[:, O_K:O_V])
    v_ref[...] = _dot(xn, win_ref[:, O_V:O_UV])
    qm = _dot(xn, win_ref[:, O_QM:O_GL])

    lane = lax.broadcasted_iota(jnp.int32, (n, 2 * SWA_HEAD_DIM), 1)
    for hq in range(SWA_HEADS):
        kvh = hq // SWA_GROUP
        win = q[:, (hq // 2) * 128:(hq // 2 + 1) * 128]
        if (hq % 2) != kvh:
            win = pltpu.roll(win, SWA_HEAD_DIM, 1)
        qs_ref[hq] = jnp.where((lane // SWA_HEAD_DIM) == kvh, win, 0.0)
    lane = lax.broadcasted_iota(jnp.int32, (n, MEM_WIDTH), 1)
    for hm in range(MEM_HEADS):
        qp_ref[hm] = jnp.where((lane // MEM_HEAD_DIM) == hm, qm, 0.0)

    u, vc = _chunk_uv(xn, win_ref, cng_ref)
    vc_ref[...] = vc
    tpos = lax.broadcasted_iota(jnp.int32, (n, CHUNK_GROUP_DIM), 0) % dec_seq
    for g in range(CHUNK_GROUPS):
        gs = slice(g * CHUNK_GROUP_DIM, (g + 1) * CHUNK_GROUP_DIM)
        vg = vc[:, gs]
        sg = jnp.zeros((n, CHUNK_GROUP_DIM), F32)
        for t in range(dec_seq):
            sg = jnp.where(tpos == t, bs4_ref[g * dec_seq + t], sg)
        for d in range(dec_seq):
            shifted = vg if d == 0 else pltpu.roll(vg, d, 0)
            coef = jnp.zeros((n, CHUNK_GROUP_DIM), F32)
            for t in range(d, dec_seq):
                coef = jnp.where(tpos == t, ws4_ref[(g * dec_seq + t) * dec_seq + (t - d)], coef)
            sg = sg + coef * shifted
        achk_ref[:, gs] = (u[:, gs] * sg).astype(BF16)


def _sample_proj_call(x2, ws4, bs4, g1, w_in_b, cng, *, dec_seq):
    n = x2.shape[0]
    full = lambda r, w: pl.BlockSpec((r, w), lambda i: (0, 0))
    planes = lambda p, w: pl.BlockSpec((p, n, w), lambda i: (0, 0, 0))
    return pl.pallas_call(
        functools.partial(_sample_proj_kernel, dec_seq=dec_seq),
        grid=(1,),
        in_specs=[_smem_spec(), _smem_spec(), full(n, D_MODEL), _const_spec((1, D_MODEL)),
                  _const_spec((D_MODEL, IN_WIDTH)), _const_spec((1, CHUNK_WIDTH))],
        out_specs=[planes(SWA_HEADS, KV_WIDTH), planes(MEM_HEADS, MEM_WIDTH),
                   full(n, KV_WIDTH), full(n, KV_WIDTH), full(n, CHUNK_WIDTH), full(n, CHUNK_WIDTH)],
        out_shape=[jax.ShapeDtypeStruct((SWA_HEADS, n, KV_WIDTH), F32),
                   jax.ShapeDtypeStruct((MEM_HEADS, n, MEM_WIDTH), F32),
                   jax.ShapeDtypeStruct((n, KV_WIDTH), F32), jax.ShapeDtypeStruct((n, KV_WIDTH), F32),
                   jax.ShapeDtypeStruct((n, CHUNK_WIDTH), F32), jax.ShapeDtypeStruct((n, CHUNK_WIDTH), BF16)],
        compiler_params=pltpu.CompilerParams(dimension_semantics=("arbitrary",), vmem_limit_bytes=VMEM_LIMIT),
        name="sample_proj",
    )(ws4, bs4, x2, g1, w_in_b, cng)


def _sample_attn_kernel(sinks_ref, qs_ref, qp_ref, kn_ref, vn_ref, ck_ref, cv_ref, cmk_ref, cmv_ref,
                        aswa_ref, amem_ref, nk_ref, nv_ref, kk_s, vv_s, km_s, vm_s, *, dec_seq):
    sb = ck_ref.shape[0]
    ntok = sb * dec_seq
    span = 2 * WINDOW
    for sl in range(sb):
        tok = slice(sl * dec_seq, (sl + 1) * dec_seq)
        base = sl * span
        for stage, cache, new, out in ((kk_s, ck_ref, kn_ref, nk_ref), (vv_s, cv_ref, vn_ref, nv_ref)):
            stage[base:base + WINDOW, :] = cache[sl]
            stage[base + WINDOW:base + WINDOW + dec_seq, :] = new[tok, :]
            stage[base + WINDOW + dec_seq:base + span, :] = jnp.zeros((WINDOW - dec_seq, KV_WIDTH), F32)
            out[sl] = stage[base + dec_seq:base + dec_seq + WINDOW, :]
        for stage, cache in ((km_s, cmk_ref), (vm_s, cmv_ref)):
            for hm in range(MEM_HEADS):
                stage[sl * MEM_LEN:(sl + 1) * MEM_LEN, hm * MEM_HEAD_DIM:(hm + 1) * MEM_HEAD_DIM] = (
                    cache[sl, pl.ds(hm, MEM_LEN, stride=MEM_HEADS), :].astype(BF16))

    def row_info(nrows):
        r = lax.broadcasted_iota(jnp.int32, (nrows, 1), 0)
        return r // ntok, (r % ntok) // dec_seq, (r % ntok) % dec_seq

    head, seq_r, t_r = row_info(SWA_HEADS * ntok)
    col = lax.broadcasted_iota(jnp.int32, (1, sb * span), 1)
    dist = (WINDOW + t_r) - col % span
    valid = (col // span == seq_r) & (dist >= 0) & (dist <= WINDOW)
    sink = jnp.zeros(head.shape, F32)
    for hq in range(SWA_HEADS):
        sink = jnp.where(head == hq, sinks_ref[hq], sink)
    qs = qs_ref[...].reshape(SWA_HEADS * ntok, KV_WIDTH).astype(BF16)
    s = _dot_nt(qs, kk_s[...].astype(BF16)) * SWA_SCALE - jnp.exp2(-(head + 1).astype(F32)) * dist.astype(F32)
    s = jnp.where(valid, s, NEG)
    m = jnp.maximum(jnp.max(s, axis=-1, keepdims=True), sink)
    e = jnp.exp(s - m)
    w = e / (jnp.sum(e, axis=-1, keepdims=True) + jnp.exp(sink - m))
    vv = vv_s[...]
    vr = pltpu.roll(vv, SWA_HEAD_DIM, 1)
    lane_kv = lax.broadcasted_iota(jnp.int32, vv.shape, 1) // SWA_HEAD_DIM
    dup0 = jnp.where(lane_kv == 0, vv, vr).astype(BF16)
    dup1 = jnp.where(lane_kv == 1, vv, vr).astype(BF16)
    o = _dot(w.astype(BF16), jnp.concatenate([dup0, dup0, dup1, dup1], axis=1))
    lane_s = lax.broadcasted_iota(jnp.int32, (ntok, SWA_WIDTH), 1) // SWA_HEAD_DIM
    acc = jnp.zeros((ntok, SWA_WIDTH), F32)
    for hq in range(SWA_HEADS):
        acc = acc + jnp.where(lane_s == hq, o[hq * ntok:(hq + 1) * ntok, :], 0.0)
    aswa_ref[...] = acc

    _, seq_m, _ = row_info(MEM_HEADS * ntok)
    colm = lax.broadcasted_iota(jnp.int32, (1, sb * MEM_LEN), 1)
    qp = qp_ref[...].reshape(MEM_HEADS * ntok, MEM_WIDTH).astype(BF16)
    sm = jnp.where(colm // MEM_LEN == seq_m, _dot_nt(qp, km_s[...]) * MEM_SCALE, NEG)
    om = _dot(_softmax_rows(sm).astype(BF16), vm_s[...])
    lane_m = lax.broadcasted_iota(jnp.int32, (ntok, MEM_WIDTH), 1) // MEM_HEAD_DIM
    accm = jnp.zeros((ntok, MEM_WIDTH), F32)
    for hm in range(MEM_HEADS):
        accm = accm + jnp.where(lane_m == hm, om[hm * ntok:(hm + 1) * ntok, :], 0.0)
    amem_ref[...] = accm


def _sample_attn_call(sinks, qs, qp, kn, vn, ck, cv, cmk, cmv, *, dec_seq):
    nseq = ck.shape[0]
    sb = SB_SAMPLE
    rows = lambda w: pl.BlockSpec((sb * dec_seq, w), lambda i: (i, 0))
    planes = lambda p, w: pl.BlockSpec((p, sb * dec_seq, w), lambda i: (0, i, 0))
    seqs = lambda r, w: pl.BlockSpec((sb, r, w), lambda i: (i, 0, 0))
    n = nseq * dec_seq
    return pl.pallas_call(
        functools.partial(_sample_attn_kernel, dec_seq=dec_seq),
        grid=(nseq // sb,),
        in_specs=[_smem_spec(), planes(SWA_HEADS, KV_WIDTH), planes(MEM_HEADS, MEM_WIDTH),
                  rows(KV_WIDTH), rows(KV_WIDTH), seqs(WINDOW, KV_WIDTH), seqs(WINDOW, KV_WIDTH),
                  seqs(MEM_LEN * MEM_HEADS, MEM_HEAD_DIM), seqs(MEM_LEN * MEM_HEADS, MEM_HEAD_DIM)],
        out_specs=[rows(SWA_WIDTH), rows(MEM_WIDTH), seqs(WINDOW, KV_WIDTH), seqs(WINDOW, KV_WIDTH)],
        out_shape=[jax.ShapeDtypeStruct((n, SWA_WIDTH), F32), jax.ShapeDtypeStruct((n, MEM_WIDTH), F32),
                   jax.ShapeDtypeStruct((nseq, WINDOW, KV_WIDTH), F32),
                   jax.ShapeDtypeStruct((nseq, WINDOW, KV_WIDTH), F32)],
        scratch_shapes=[pltpu.VMEM((sb * 2 * WINDOW, KV_WIDTH), F32), pltpu.VMEM((sb * 2 * WINDOW, KV_WIDTH), F32),
                        pltpu.VMEM((sb * MEM_LEN, MEM_WIDTH), BF16), pltpu.VMEM((sb * MEM_LEN, MEM_WIDTH), BF16)],
        compiler_params=pltpu.CompilerParams(dimension_semantics=("arbitrary",), vmem_limit_bytes=VMEM_LIMIT),
        name="sample_attn",
    )(sinks, qs, qp, kn, vn, ck, cv, cmk, cmv)


def _sample_merge_kernel(x_ref, aswa_ref, achk_ref, amem_ref, g1_ref, win_ref, bg_ref, wbo_ref, wout_ref, g2_ref,
                         h_ref, xn2_ref):
    x = x_ref[...]
    xn = _rms(x, g1_ref[...]).astype(BF16)
    _merge_tail(x, xn, (aswa_ref[...].astype(BF16), achk_ref[...], amem_ref[...].astype(BF16)),
                win_ref, bg_ref, wbo_ref, wout_ref, g2_ref, h_ref, xn2_ref)


def _sample_merge_call(x2, aswa, achk, amem, g1, w_in_b, bg, wbo_b, wout_b, g2):
    n = x2.shape[0]
    full = lambda w: pl.BlockSpec((n, w), lambda i: (0, 0))
    return pl.pallas_call(
        _sample_merge_kernel,
        grid=(1,),
        in_specs=[full(D_MODEL), full(BRANCH_WIDTH), full(BRANCH_WIDTH), full(BRANCH_WIDTH),
                  _const_spec((1, D_MODEL)), _const_spec((D_MODEL, IN_WIDTH)), _const_spec((N_BRANCH, D_MODEL)),
                  _const_spec((N_BRANCH, BRANCH_WIDTH, D_MODEL)), _const_spec((D_MODEL, D_MODEL)),
                  _const_spec((1, D_MODEL))],
        out_specs=[full(D_MODEL), full(D_MODEL)],
        out_shape=[jax.ShapeDtypeStruct((n, D_MODEL), F32), jax.ShapeDtypeStruct((n, D_MODEL), BF16)],
        compiler_params=pltpu.CompilerParams(dimension_semantics=("arbitrary",), vmem_limit_bytes=VMEM_LIMIT),
        name="sample_merge",
    )(x2, aswa, achk, amem, g1, w_in_b, bg, wbo_b, wout_b, g2)


def _expert_tables_kernel(pu_ref, pv_ref, pub_ref, pvt_ref):
    pub_ref[...] = pu_ref[...].astype(BF16)
    pvt_ref[...] = pv_ref[...].T.astype(BF16)


def _expert_tables_call(peer_u, peer_v):
    _, ne, d = peer_u.shape
    rows = EB_DENSE
    blk = pl.BlockSpec((None, rows, d), lambda i: (0, i, 0))
    return pl.pallas_call(
        _expert_tables_kernel,
        grid=(ne // rows,),
        in_specs=[blk, blk],
        out_specs=[pl.BlockSpec((rows, d), lambda i: (i, 0)), pl.BlockSpec((None, d, rows), lambda i: (i, 0, 0))],
        out_shape=[jax.ShapeDtypeStruct((ne, d), BF16), jax.ShapeDtypeStruct((ne // rows, d, rows), BF16)],
        compiler_params=pltpu.CompilerParams(dimension_semantics=("arbitrary",), vmem_limit_bytes=VMEM_LIMIT),
        name="expert_tables",
    )(peer_u, peer_v)


def _top16(s, key_iota):
    rank = jnp.full(s.shape, float(PEER_TOPK), F32)
    vals = []
    for kth in range(PEER_TOPK):
        m = jnp.max(s, axis=0, keepdims=True)
        idx = jnp.min(jnp.where(s == m, key_iota, float(PEER_N_KEYS)), axis=0, keepdims=True)
        hit = key_iota == idx
        rank = jnp.where(hit, float(kth), rank)
        s = jnp.where(hit, -jnp.inf, s)
        vals.append(m)
    return rank, vals


def _route_exact(s1, s2):
    lanes = s1.shape[1]
    key_iota = lax.broadcasted_iota(jnp.int32, (PEER_N_KEYS, lanes), 0).astype(F32)
    cand_iota = lax.broadcasted_iota(jnp.int32, (PEER_TOPK * PEER_TOPK, lanes), 0).astype(F32)
    k16 = lax.broadcasted_iota(jnp.int32, (PEER_TOPK, lanes), 0)
    rank1, v1 = _top16(s1, key_iota)
    rank2, v2 = _top16(s2, key_iota)
    top2 = jnp.zeros((PEER_TOPK, lanes), F32)
    for b in range(PEER_TOPK):
        top2 = jnp.where(k16 == b, v2[b], top2)
    cand = jnp.concatenate([v1[a] + top2 for a in range(PEER_TOPK)], axis=0)
    sel = jnp.zeros(cand.shape, F32)
    cmax = v1[0] + v2[0]
    z = jnp.zeros((1, lanes), F32)
    for _ in range(PEER_TOPK):
        m = jnp.max(cand, axis=0, keepdims=True)
        idx = jnp.min(jnp.where(cand == m, cand_iota, float(PEER_TOPK * PEER_TOPK)), axis=0, keepdims=True)
        hit = cand_iota == idx
        sel = jnp.where(hit, 1.0, sel)
        cand = jnp.where(hit, -jnp.inf, cand)
        z = z + jnp.exp(m - cmax)
    cnt = jnp.zeros((PEER_N_KEYS, lanes), F32)
    for a in range(PEER_TOPK):
        n_a = jnp.sum(sel[a * PEER_TOPK:(a + 1) * PEER_TOPK], axis=0, keepdims=True)
        cnt = jnp.where(rank1 == float(a), n_a, cnt)
    return cnt, rank2, jnp.exp(s1 - v1[0]), jnp.exp(s2 - v2[0]) * (GELU_HALF / z)


def _hi_lo(a, b):
    if a is None or b is None:
        return (b if a is None else a), None
    return jnp.maximum(a, b), jnp.minimum(a, b)


def _sort16_desc(v):
    n = len(v)
    k = 2
    while k <= n:
        j = k // 2
        while j >= 1:
            for i in range(n):
                l = i ^ j
                if l > i:
                    hi, lo = _hi_lo(v[i], v[l])
                    v[i], v[l] = (hi, lo) if (i & k) == 0 else (lo, hi)
            j //= 2
        k *= 2


def _merge16_desc(v):
    j = len(v) // 2
    while j >= 1:
        for i in range(len(v)):
            l = i ^ j
            if l > i:
                v[i], v[l] = _hi_lo(v[i], v[l])
        j //= 2


def _sublane_allsum(x):
    for shift in (4, 2, 1):
        x = x + pltpu.roll(x, shift, 0)
    return x


def _sorted_top16(tiles):
    v = list(tiles)
    _sort16_desc(v)
    for shift in (4, 2, 1):
        partner = [None if x is None else pltpu.roll(x, shift, 0) for x in reversed(v)]
        v = [_hi_lo(a, b)[0] for a, b in zip(v, partner)]
        _merge16_desc(v)
    return v


def _route_fast(s1, s2):
    nt = PEER_N_KEYS // 8
    lanes = s1.shape[1]
    sub = lax.broadcasted_iota(jnp.int32, (8, lanes), 0)
    ninf = jnp.full((8, lanes), -jnp.inf, F32)
    zero = jnp.zeros((8, lanes), F32)
    t1 = [s1[r * 8:(r + 1) * 8, :] for r in range(nt)]
    t2 = [s2[r * 8:(r + 1) * 8, :] for r in range(nt)]
    v1 = _sorted_top16(t1)
    v2 = _sorted_top16(t2)

    rank2 = []
    for x in t2:
        r = jnp.full((8, lanes), float(PEER_TOPK), F32)
        for k in range(PEER_TOPK - 1, -1, -1):
            r = jnp.where(x >= v2[k], float(k), r)
        rank2.append(r)

    def rows(v, base):
        out = v[base]
        for b in range(1, 8):
            out = jnp.where(sub == b, v[base + b], out)
        return out

    r1 = [rows(v1, 0), rows(v1, 8)]
    r2 = [rows(v2, 0), rows(v2, 8)]
    cands = [v1[0] + r2[0], v1[0] + r2[1], v1[1] + r2[0],
             jnp.where(sub < 5, v1[2] + r2[0], ninf), jnp.where(sub < 4, v1[3] + r2[0], ninf),
             jnp.where(sub >= 4, v2[0] + r1[0], ninf), v2[0] + r1[1],
             jnp.where(sub >= 4, v2[1] + r1[0], ninf), jnp.where(sub == 4, v2[2] + r1[0], ninf)]
    top = _sorted_top16(cands + [None] * (PEER_TOPK - len(cands)))
    tau, cmax = top[PEER_TOPK - 1], top[0]
    z = zero
    for c in cands:
        z = z + jnp.where(c >= tau, jnp.exp(c - cmax), 0.0)
    z = _sublane_allsum(z)

    cnt = []
    for x in t1:
        c = zero
        for b in range(3):
            c = jnp.where(x + v2[b] >= tau, float(b + 1), c)
        cnt.append(c)
    for a in range(4):
        n_a = zero
        for b in range(PEER_TOPK // (a + 1)):
            n_a = jnp.where(v1[a] + v2[b] >= tau, float(b + 1), n_a)
        cnt = [jnp.where(x == v1[a], n_a, c) for x, c in zip(t1, cnt)]

    total = _sublane_allsum(functools.reduce(lambda a, b: a + b, cnt))
    ranked = _sublane_allsum(functools.reduce(lambda a, b: a + b,
                                              [jnp.where(r < float(PEER_TOPK), 1.0, 0.0) for r in rank2]))
    bad = jnp.where(total != float(PEER_TOPK), 1.0, 0.0) + jnp.where(ranked != float(PEER_TOPK), 1.0, 0.0)
    for k in range(PEER_TOPK - 1):
        bad = bad + jnp.where(v2[k] == v2[k + 1], 1.0, 0.0)
    for k in range(4):
        bad = bad + jnp.where(v1[k] == v1[k + 1], 1.0, 0.0)

    rz = GELU_HALF / z
    cat = lambda tiles: jnp.concatenate(tiles, axis=0)
    return (cat(cnt), cat(rank2), cat([jnp.exp(x - v1[0]) for x in t1]),
            cat([jnp.exp(x - v2[0]) * rz for x in t2]), bad)


def _peer_route_kernel(xn2_ref, wqt_ref, sub_ref, cnt_ref, rnk_ref, c1_ref, e2_ref, sc_s):
    tp = xn2_ref.shape[0]
    lanes = 128

    qt = _dot_nt(wqt_ref[...], xn2_ref[...]).astype(BF16)
    for hp in range(2 * PEER_HEADS):
        sc_s[hp] = _dot(sub_ref[hp % 2], qt[hp * PEER_HALF:(hp + 1) * PEER_HALF, :])

    def head_body(h, carry):
        for lg in range(tp // lanes):
            cols = slice(lg * lanes, (lg + 1) * lanes)
            s1 = sc_s[h * 2, :, cols]
            s2 = sc_s[h * 2 + 1, :, cols]

            def store(cnt, rank2, c1, e2):
                cnt_ref[h, :, cols] = cnt
                rnk_ref[h, :, cols] = rank2.astype(BF16)
                c1_ref[h, :, cols] = c1
                e2_ref[h, :, cols] = e2.astype(BF16)

            *tables, bad = _route_fast(s1, s2)
            store(*tables)

            @pl.when(jnp.max(bad) > 0.0)
            def _():
                store(*_route_exact(s1, s2))
        return carry

    lax.fori_loop(0, PEER_HEADS, head_body, 0)


def _peer_route_call(xn2, wqt_b, sub_b):
    n = xn2.shape[0]
    tp = TP_PRE
    tab = pl.BlockSpec((PEER_HEADS, PEER_N_KEYS, tp), lambda i: (0, 0, i))
    return pl.pallas_call(
        _peer_route_kernel,
        grid=(n // tp,),
        in_specs=[pl.BlockSpec((tp, D_MODEL), lambda i: (i, 0)),
                  _const_spec((PEER_HEADS * 2 * PEER_HALF, D_MODEL)), _const_spec((2, PEER_N_KEYS, PEER_HALF))],
        out_specs=[tab] * 4,
        out_shape=[jax.ShapeDtypeStruct((PEER_HEADS, PEER_N_KEYS, n), dt) for dt in (F32, BF16, F32, BF16)],
        scratch_shapes=[pltpu.VMEM((2 * PEER_HEADS, PEER_N_KEYS, tp), F32)],
        compiler_params=pltpu.CompilerParams(dimension_semantics=("arbitrary",), vmem_limit_bytes=VMEM_LIMIT),
        name="peer_route",
    )(xn2, wqt_b, sub_b)


def _peer_dense_kernel(xn2_ref, pu_ref, pvt_ref, cnt_ref, c1_ref, rnk_ref, e2_ref, h_ref, gf_ref,
                       y_ref, acc_ref, at_s, g_s, rnk_s, e2_s, xt_s):
    k = pl.program_id(1)
    kps = 4
    eb = kps * PEER_N_KEYS
    tm = xn2_ref.shape[0]
    sub_rows = 16
    ntile = PEER_N_KEYS // sub_rows

    @pl.when(k == 0)
    def _():
        acc_ref[...] = jnp.zeros_like(acc_ref)
        rnk_s[...] = rnk_ref[...]
        e2_s[...] = e2_ref[...]
        xt_s[...] = xn2_ref[...].astype(F32).T.astype(BF16)

    def weights_and_gate(row0, base):
        cnt_rows = [[cnt_ref[h, row0 + ii:row0 + ii + 1, :].astype(BF16) for ii in range(kps)]
                    for h in range(PEER_HEADS)]
        c1_rows = [[c1_ref[h, row0 + ii:row0 + ii + 1, :].astype(BF16) for ii in range(kps)]
                   for h in range(PEER_HEADS)]
        for cg in range(tm // 128):
            cols = slice(cg * 128, (cg + 1) * 128)
            w = [[None] * ntile for _ in range(kps)]
            for h in range(PEER_HEADS):
                cnt_b = [jnp.broadcast_to(cnt_rows[h][ii][:, cols], (sub_rows, 128)) for ii in range(kps)]
                c1_b = [jnp.broadcast_to(c1_rows[h][ii][:, cols], (sub_rows, 128)) for ii in range(kps)]
                for r in range(ntile):
                    jr = slice(r * sub_rows, (r + 1) * sub_rows)
                    rt = rnk_s[h, jr, cols]
                    et = e2_s[h, jr, cols]
                    for ii in range(kps):
                        term = jnp.where(rt < cnt_b[ii], et, 0.0) * c1_b[ii]
                        w[ii][r] = term if w[ii][r] is None else w[ii][r] + term
            for ii in range(kps):
                for r in range(ntile):
                    e0 = base + ii * PEER_N_KEYS + r * sub_rows
                    er = slice(e0, e0 + sub_rows)
                    a = at_s[er, cols]
                    g_s[er, cols] = (a * (1.0 + lax.erf(a * (2.0 ** -0.5)))).astype(BF16) * w[ii][r]

    at_s[...] = _dot(pu_ref[...], xt_s[...])
    for b in range(at_s.shape[0] // eb):
        weights_and_gate(b * kps, b * eb)
    acc_ref[...] += _dot(pvt_ref[...], g_s[...])

    @pl.when(k == pl.num_programs(1) - 1)
    def _():
        y_ref[...] = _rms(h_ref[...] + acc_ref[...].T, gf_ref[...])


def _peer_dense_call(xn2, pu_b, pvt_b, cnt, rnk, c1, e2, h, gf):
    n = xn2.shape[0]
    tm, eb = min(TM_DENSE, n), EB_DENSE
    nstep = PEER_N_EXPERTS // eb
    tok = lambda dt: pl.BlockSpec((tm, D_MODEL), lambda t, k: (t, 0))
    tab = pl.BlockSpec((PEER_HEADS, PEER_N_KEYS, tm), lambda t, k: (0, 0, t))
    rows = pl.BlockSpec((PEER_HEADS, eb // PEER_N_KEYS, tm), lambda t, k: (0, k, t))
    return pl.pallas_call(
        _peer_dense_kernel,
        grid=(n // tm, nstep),
        in_specs=[tok(BF16), pl.BlockSpec((eb, D_MODEL), lambda t, k: (k, 0)),
                  pl.BlockSpec((None, D_MODEL, eb), lambda t, k: (k, 0, 0)),
                  rows, rows, tab, tab, tok(F32), pl.BlockSpec((1, D_MODEL), lambda t, k: (0, 0))],
        out_specs=tok(F32),
        out_shape=jax.ShapeDtypeStruct((n, D_MODEL), F32),
        scratch_shapes=[pltpu.VMEM((D_MODEL, tm), F32), pltpu.VMEM((eb, tm), F32), pltpu.VMEM((eb, tm), BF16),
                        pltpu.VMEM((PEER_HEADS, PEER_N_KEYS, tm), BF16),
                        pltpu.VMEM((PEER_HEADS, PEER_N_KEYS, tm), BF16),
                        pltpu.VMEM((D_MODEL, tm), BF16)],
        compiler_params=pltpu.CompilerParams(dimension_semantics=("arbitrary", "arbitrary"),
                                             vmem_limit_bytes=VMEM_LIMIT),
        name="peer_dense",
    )(xn2, pu_b, pvt_b, cnt, c1, rnk, e2, h, gf)


def _peer_and_final_norm(h, xn2, wqt_b, sub_b, pu_b, pvt_b, gf):
    cnt, rnk, c1, e2 = _peer_route_call(xn2, wqt_b, sub_b)
    return _peer_dense_call(xn2, pu_b, pvt_b, cnt, rnk, c1, e2, h, gf)


def kernel(x_prompt, x_sample, cache_swa_k, cache_swa_v, cache_mem_k, cache_mem_v, mem_prompt,
           norm1_g, w_in, b_gate, swa_sinks, chunk_norm_g, chunk_ws, chunk_bs, mem_norm_g, w_mem_kv,
           w_branch_out, w_out, norm2_g, peer_wq, peer_subkeys, peer_u, peer_v, final_norm_g):
    depth = norm1_g.shape[0]
    assert depth == 1, "single-layer step"
    batch, seq, _ = x_prompt.shape
    nseq, dec_seq, _ = x_sample.shape

    g1 = norm1_g[0][None]
    g2 = norm2_g[0][None]
    gf = final_norm_g[None]
    cng = chunk_norm_g[0][None]
    w_in_b = w_in[0].astype(BF16)
    wbo_b = w_branch_out[0].astype(BF16)
    wout_b = w_out[0].astype(BF16)
    wqt_b = peer_wq[0].T.astype(BF16)
    sub_b = peer_subkeys[0].astype(BF16)
    pu_b, pvt_b = _expert_tables_call(peer_u, peer_v)
    sinks = swa_sinks[0]
    bst = chunk_bs[0].T
    ws4 = chunk_ws[0][:, :dec_seq, :dec_seq].reshape(-1)
    bs4 = chunk_bs[0][:, :dec_seq].reshape(-1)

    mk, mv, mkb, mvb = _memkv_call(mem_prompt, mem_norm_g[0][None], w_mem_kv[0].astype(BF16))
    hp, xn2p, kp, vp = _prompt_layer_call(
        x_prompt.reshape(batch * seq, D_MODEL), mkb, mvb, sinks, g1, w_in_b, cng, chunk_ws[0], bst,
        b_gate[0], wbo_b, wout_b, g2, batch=batch, seq=seq)
    y_prompt = _peer_and_final_norm(hp, xn2p, wqt_b, sub_b, pu_b, pvt_b, gf).reshape(batch, seq, D_MODEL)

    xs = x_sample.reshape(nseq * dec_seq, D_MODEL)
    qs, qp, kn, vn, vcs, achk = _sample_proj_call(xs, ws4, bs4, g1, w_in_b, cng, dec_seq=dec_seq)
    aswa, amem, nk, nv = _sample_attn_call(
        sinks, qs, qp, kn, vn,
        cache_swa_k.reshape(nseq, WINDOW, KV_WIDTH), cache_swa_v.reshape(nseq, WINDOW, KV_WIDTH),
        cache_mem_k.reshape(nseq, MEM_LEN * MEM_HEADS, MEM_HEAD_DIM),
        cache_mem_v.reshape(nseq, MEM_LEN * MEM_HEADS, MEM_HEAD_DIM),
        dec_seq=dec_seq)
    hs, xn2s = _sample_merge_call(xs, aswa, achk, amem, g1, w_in_b, b_gate[0], wbo_b, wout_b, g2)
    y_sample = _peer_and_final_norm(hs, xn2s, wqt_b, sub_b, pu_b, pvt_b, gf).reshape(nseq, dec_seq, D_MODEL)

    kv5 = lambda a, b: a.reshape(1, b, WINDOW, SWA_KV_HEADS, SWA_HEAD_DIM)
    mem5 = lambda a: a.reshape(1, batch, MEM_LEN, MEM_HEADS, MEM_HEAD_DIM)
    return (y_prompt, y_sample, kv5(kp, batch), kv5(vp, batch), mem5(mk), mem5(mv),
            kv5(nk, nseq), kv5(nv, nseq), vcs.reshape(1, nseq, dec_seq, CHUNK_GROUPS, CHUNK_GROUP_DIM))
```

```python
import functools

import jax
import jax.numpy as jnp
from jax import lax
from jax.experimental import pallas as pl
from jax.experimental.pallas import tpu as pltpu

F32 = jnp.float32
BF16 = jnp.bfloat16

D_MODEL = 1024
SWA_HEADS = 8
SWA_KV_HEADS = 2
SWA_GROUP = SWA_HEADS // SWA_KV_HEADS
SWA_HEAD_DIM = 64
SWA_WIDTH = SWA_HEADS * SWA_HEAD_DIM
KV_WIDTH = SWA_KV_HEADS * SWA_HEAD_DIM
WINDOW = 128
SWA_SCALE = SWA_HEAD_DIM ** -0.5
CHUNK = 128
CHUNK_GROUPS = 4
CHUNK_GROUP_DIM = 128
CHUNK_WIDTH = CHUNK_GROUPS * CHUNK_GROUP_DIM
MEM_LEN = 256
MEM_HEADS = 4
MEM_HEAD_DIM = 128
MEM_WIDTH = MEM_HEADS * MEM_HEAD_DIM
MEM_SCALE = MEM_HEAD_DIM ** -0.5
N_BRANCH = 3
BRANCH_WIDTH = 512
PEER_HEADS = 8
PEER_N_KEYS = 128
PEER_N_EXPERTS = PEER_N_KEYS * PEER_N_KEYS
PEER_HALF = 128
PEER_TOPK = 16
EPS = 1e-6
NEG = -1e30

O_Q = 0
O_K = O_Q + SWA_WIDTH
O_V = O_K + KV_WIDTH
O_UV = O_V + KV_WIDTH
O_QM = O_UV + 2 * CHUNK_WIDTH
O_GL = O_QM + MEM_WIDTH
IN_WIDTH = O_GL + N_BRANCH * D_MODEL

LANES = 128
SUBLANES = 8
BF16_TILE_ROWS = 16

TM_LAYER = 512
TP_PRE = 512
TM_DENSE = 512
EB_DENSE = 2048
SB_SAMPLE = 8
VMEM_LIMIT = 56 * 1024 * 1024

_NT = (((1,), (1,)), ((), ()))
GELU_HALF = 0.5


def _rms(x, g):
    return x * lax.rsqrt(jnp.mean(x * x, axis=-1, keepdims=True) + EPS) * g


def _gelu(x):
    return 0.5 * x * (1.0 + lax.erf(x * (2.0 ** -0.5)))


def _dot(a, b):
    return jnp.dot(a, b, preferred_element_type=F32)


def _dot_nt(a, b):
    return lax.dot_general(a, b, _NT, preferred_element_type=F32)


def _const_spec(shape):
    nd = len(shape)
    return pl.BlockSpec(shape, lambda *_: (0,) * nd, pipeline_mode=pl.Buffered(1))


def _smem_spec():
    return pl.BlockSpec(memory_space=pltpu.SMEM)


def _chunk_uv(xn, win_ref, cng_ref):
    uv = _gelu(_dot(xn, win_ref[:, O_UV:O_QM]))
    return uv[:, :CHUNK_WIDTH], _rms(uv[:, CHUNK_WIDTH:], cng_ref[...])


def _merge_tail(x, xn, branches, win_ref, bg_ref, wbo_ref, wout_ref, g2_ref, h_ref, xn2_ref):
    acc = None
    for j in range(N_BRANCH):
        pj = _dot(branches[j], wbo_ref[j])
        gl = _dot(xn, win_ref[:, O_GL + j * D_MODEL:O_GL + (j + 1) * D_MODEL]) + bg_ref[j:j + 1, :]
        term = jax.nn.sigmoid(gl) * pj
        acc = term if acc is None else acc + term
    h = x + _dot(acc.astype(BF16), wout_ref[...])
    h_ref[...] = h
    xn2_ref[...] = _rms(h, g2_ref[...]).astype(BF16)


def _softmax_rows(s):
    m = jnp.max(s, axis=-1, keepdims=True)
    e = jnp.exp(s - m)
    return e / jnp.sum(e, axis=-1, keepdims=True)


def _memkv_kernel(mem_ref, g_ref, w_ref, k_ref, v_ref, kb_ref, vb_ref):
    kv = _dot(_rms(mem_ref[...], g_ref[...]).astype(BF16), w_ref[...])
    k = kv[:, :MEM_WIDTH]
    v = kv[:, MEM_WIDTH:]
    k_ref[...] = k
    v_ref[...] = v
    kb_ref[...] = k.astype(BF16)
    vb_ref[...] = v.astype(BF16)


def _memkv_call(mem, g, w_b):
    b, m, d = mem.shape
    blk = lambda w: pl.BlockSpec((None, m, w), lambda i: (i, 0, 0))
    return pl.pallas_call(
        _memkv_kernel,
        grid=(b,),
        in_specs=[blk(d), _const_spec((1, d)), _const_spec((d, 2 * MEM_WIDTH))],
        out_specs=[blk(MEM_WIDTH)] * 4,
        out_shape=[jax.ShapeDtypeStruct((b, m, MEM_WIDTH), F32)] * 2
        + [jax.ShapeDtypeStruct((b, m, MEM_WIDTH), BF16)] * 2,
        compiler_params=pltpu.CompilerParams(dimension_semantics=("arbitrary",), vmem_limit_bytes=VMEM_LIMIT),
        name="mem_kv",
    )(mem, g, w_b)


def _prompt_layer_kernel(sinks_ref, x_ref, mk_ref, mv_ref, g1_ref, win_ref, cng_ref, ws_ref, bst_ref,
                         bg_ref, wbo_ref, wout_ref, g2_ref,
                         h_ref, xn2_ref, kout_ref, vout_ref,
                         kprev_ref, vprev_ref, aswa_ref, achk_ref, amem_ref, *, tiles_per_batch):
    tm = x_ref.shape[0]
    nblk = tm // WINDOW
    i = pl.program_id(0)
    first = (i % tiles_per_batch) == 0

    @pl.when(first)
    def _():
        kprev_ref[...] = jnp.zeros_like(kprev_ref)
        vprev_ref[...] = jnp.zeros_like(vprev_ref)

    x = x_ref[...]
    xn = _rms(x, g1_ref[...]).astype(BF16)

    q = _dot(xn, win_ref[:, O_Q:O_K]).astype(BF16)
    k = _dot(xn, win_ref[:, O_K:O_V])
    v = _dot(xn, win_ref[:, O_V:O_UV])
    kout_ref[...] = k[tm - WINDOW:]
    vout_ref[...] = v[tm - WINDOW:]
    kb = k.astype(BF16)
    vb = v.astype(BF16)
    r = lax.broadcasted_iota(jnp.int32, (WINDOW, 2 * WINDOW), 0)
    c = lax.broadcasted_iota(jnp.int32, (WINDOW, 2 * WINDOW), 1)
    dist = WINDOW + r - c
    band = (dist >= 0) & (dist <= WINDOW)
    distf = dist.astype(F32)
    first_lo = jnp.where(first, WINDOW, 0)
    for n in range(nblk):
        rows = slice(n * WINDOW, (n + 1) * WINDOW)
        if n == 0:
            kp, vp = kprev_ref[...], vprev_ref[...]
            valid = band & (c >= first_lo)
        else:
            prev = slice((n - 1) * WINDOW, n * WINDOW)
            kp, vp = kb[prev], vb[prev]
            valid = band
        kk = jnp.concatenate([kp, kb[rows]], axis=0)
        vv = jnp.concatenate([vp, vb[rows]], axis=0)
        for hq in range(SWA_HEADS):
            kvs = slice((hq // SWA_GROUP) * SWA_HEAD_DIM, (hq // SWA_GROUP + 1) * SWA_HEAD_DIM)
            hs = slice(hq * SWA_HEAD_DIM, (hq + 1) * SWA_HEAD_DIM)
            s = _dot_nt(q[rows, hs], kk[:, kvs]) * SWA_SCALE - (2.0 ** -(hq + 1)) * distf
            s = jnp.where(valid, s, NEG)
            sink = sinks_ref[hq]
            m = jnp.maximum(jnp.max(s, axis=-1, keepdims=True), sink)
            e = jnp.exp(s - m)
            w = e / (jnp.sum(e, axis=-1, keepdims=True) + jnp.exp(sink - m))
            aswa_ref[rows, hs] = _dot(w.astype(BF16), vv[:, kvs]).astype(BF16)
    kprev_ref[...] = kb[tm - WINDOW:]
    vprev_ref[...] = vb[tm - WINDOW:]

    u, vc = _chunk_uv(xn, win_ref, cng_ref)
    vcb = vc.astype(BF16)
    tri = (lax.broadcasted_iota(jnp.int32, (CHUNK, CHUNK), 0)
           >= lax.broadcasted_iota(jnp.int32, (CHUNK, CHUNK), 1))
    for g in range(CHUNK_GROUPS):
        gs = slice(g * CHUNK_GROUP_DIM, (g + 1) * CHUNK_GROUP_DIM)
        wg = jnp.where(tri, ws_ref[g], 0.0).astype(BF16)
        for n in range(nblk):
            rows = slice(n * CHUNK, (n + 1) * CHUNK)
            sg = _dot(wg, vcb[rows, gs]) + bst_ref[:, g:g + 1]
            achk_ref[rows, gs] = (u[rows, gs] * sg).astype(BF16)

    qm = _dot(xn, win_ref[:, O_QM:O_GL]).astype(BF16)
    for hm in range(MEM_HEADS):
        hs = slice(hm * MEM_HEAD_DIM, (hm + 1) * MEM_HEAD_DIM)
        w = _softmax_rows(_dot_nt(qm[:, hs], mk_ref[:, hs]) * MEM_SCALE)
        amem_ref[:, hs] = _dot(w.astype(BF16), mv_ref[:, hs]).astype(BF16)

    _merge_tail(x, xn, (aswa_ref[...], achk_ref[...], amem_ref[...]),
                win_ref, bg_ref, wbo_ref, wout_ref, g2_ref, h_ref, xn2_ref)


def _prompt_layer_call(x2, mkb, mvb, sinks, g1, w_in_b, cng, ws, bst, bg, wbo_b, wout_b, g2, *, batch, seq):
    n = x2.shape[0]
    tm = TM_LAYER
    tpb = seq // tm
    tile = lambda w: pl.BlockSpec((tm, w), lambda i: (i, 0))
    per_batch = lambda r, w: pl.BlockSpec((None, r, w), lambda i: (i // tpb, 0, 0))
    return pl.pallas_call(
        functools.partial(_prompt_layer_kernel, tiles_per_batch=tpb),
        grid=(n // tm,),
        in_specs=[_smem_spec(), tile(D_MODEL), per_batch(MEM_LEN, MEM_WIDTH), per_batch(MEM_LEN, MEM_WIDTH),
                  _const_spec((1, D_MODEL)), _const_spec((D_MODEL, IN_WIDTH)), _const_spec((1, CHUNK_WIDTH)),
                  _const_spec((CHUNK_GROUPS, CHUNK, CHUNK)), _const_spec((CHUNK, CHUNK_GROUPS)),
                  _const_spec((N_BRANCH, D_MODEL)), _const_spec((N_BRANCH, BRANCH_WIDTH, D_MODEL)),
                  _const_spec((D_MODEL, D_MODEL)), _const_spec((1, D_MODEL))],
        out_specs=[tile(D_MODEL), tile(D_MODEL), per_batch(WINDOW, KV_WIDTH), per_batch(WINDOW, KV_WIDTH)],
        out_shape=[jax.ShapeDtypeStruct((n, D_MODEL), F32), jax.ShapeDtypeStruct((n, D_MODEL), BF16),
                   jax.ShapeDtypeStruct((batch, WINDOW, KV_WIDTH), F32),
                   jax.ShapeDtypeStruct((batch, WINDOW, KV_WIDTH), F32)],
        scratch_shapes=[pltpu.VMEM((WINDOW, KV_WIDTH), BF16), pltpu.VMEM((WINDOW, KV_WIDTH), BF16),
                        pltpu.VMEM((tm, BRANCH_WIDTH), BF16), pltpu.VMEM((tm, BRANCH_WIDTH), BF16),
                        pltpu.VMEM((tm, BRANCH_WIDTH), BF16)],
        compiler_params=pltpu.CompilerParams(dimension_semantics=("arbitrary",), vmem_limit_bytes=VMEM_LIMIT),
        name="prompt_layer",
    )(sinks, x2, mkb, mvb, g1, w_in_b, cng, ws, bst, bg, wbo_b, wout_b, g2)


def _sample_proj_kernel(ws4_ref, bs4_ref, x_ref, g1_ref, win_ref, cng_ref,
                        qs_ref, qp_ref, k_ref, v_ref, vc_ref, achk_ref, *, dec_seq):
    n = x_ref.shape[0]
    xn = _rms(x_ref[...], g1_ref[...]).astype(BF16)
    q = _dot(xn, win_ref[:, O_Q:O_K])
    k_ref[...] = _dot(xn, win_ref[:, O_K:O_V])
    v_ref[...] = _dot(xn, win_ref[:, O_V:O_UV])
    qm = _dot(xn, win_ref[:, O_QM:O_GL])

    lane = lax.broadcasted_iota(jnp.int32, (n, 2 * SWA_HEAD_DIM), 1)
    for hq in range(SWA_HEADS):
        kvh = hq // SWA_GROUP
        win = q[:, (hq // 2) * LANES:(hq // 2 + 1) * LANES]
        if (hq % 2) != kvh:
            win = pltpu.roll(win, SWA_HEAD_DIM, 1)
        qs_ref[hq] = jnp.where((lane // SWA_HEAD_DIM) == kvh, win, 0.0)
    lane = lax.broadcasted_iota(jnp.int32, (n, MEM_WIDTH), 1)
    for hm in range(MEM_HEADS):
        qp_ref[hm] = jnp.where((lane // MEM_HEAD_DIM) == hm, qm, 0.0)

    u, vc = _chunk_uv(xn, win_ref, cng_ref)
    vc_ref[...] = vc
    tpos = lax.broadcasted_iota(jnp.int32, (n, CHUNK_GROUP_DIM), 0) % dec_seq
    for g in range(CHUNK_GROUPS):
        gs = slice(g * CHUNK_GROUP_DIM, (g + 1) * CHUNK_GROUP_DIM)
        vg = vc[:, gs]
        sg = jnp.zeros((n, CHUNK_GROUP_DIM), F32)
        for t in range(dec_seq):
            sg = jnp.where(tpos == t, bs4_ref[g * dec_seq + t], sg)
        for d in range(dec_seq):
            shifted = vg if d == 0 else pltpu.roll(vg, d, 0)
            coef = jnp.zeros((n, CHUNK_GROUP_DIM), F32)
            for t in range(d, dec_seq):
                coef = jnp.where(tpos == t, ws4_ref[(g * dec_seq + t) * dec_seq + (t - d)], coef)
            sg = sg + coef * shifted
        achk_ref[:, gs] = (u[:, gs] * sg).astype(BF16)


def _sample_proj_call(x2, ws4, bs4, g1, w_in_b, cng, *, dec_seq):
    n = x2.shape[0]
    full = lambda r, w: pl.BlockSpec((r, w), lambda i: (0, 0))
    planes = lambda p, w: pl.BlockSpec((p, n, w), lambda i: (0, 0, 0))
    return pl.pallas_call(
        functools.partial(_sample_proj_kernel, dec_seq=dec_seq),
        grid=(1,),
        in_specs=[_smem_spec(), _smem_spec(), full(n, D_MODEL), _const_spec((1, D_MODEL)),
                  _const_spec((D_MODEL, IN_WIDTH)), _const_spec((1, CHUNK_WIDTH))],
        out_specs=[planes(SWA_HEADS, KV_WIDTH), planes(MEM_HEADS, MEM_WIDTH),
                   full(n, KV_WIDTH), full(n, KV_WIDTH), full(n, CHUNK_WIDTH), full(n, CHUNK_WIDTH)],
        out_shape=[jax.ShapeDtypeStruct((SWA_HEADS, n, KV_WIDTH), F32),
                   jax.ShapeDtypeStruct((MEM_HEADS, n, MEM_WIDTH), F32),
                   jax.ShapeDtypeStruct((n, KV_WIDTH), F32), jax.ShapeDtypeStruct((n, KV_WIDTH), F32),
                   jax.ShapeDtypeStruct((n, CHUNK_WIDTH), F32), jax.ShapeDtypeStruct((n, CHUNK_WIDTH), BF16)],
        compiler_params=pltpu.CompilerParams(dimension_semantics=("arbitrary",), vmem_limit_bytes=VMEM_LIMIT),
        name="sample_proj",
    )(ws4, bs4, x2, g1, w_in_b, cng)


def _sample_attn_kernel(sinks_ref, qs_ref, qp_ref, kn_ref, vn_ref, ck_ref, cv_ref, cmk_ref, cmv_ref,
                        aswa_ref, amem_ref, nk_ref, nv_ref, kk_s, vv_s, km_s, vm_s, *, dec_seq):
    sb = ck_ref.shape[0]
    ntok = sb * dec_seq
    span = 2 * WINDOW
    for sl in range(sb):
        tok = slice(sl * dec_seq, (sl + 1) * dec_seq)
        base = sl * span
        for stage, cache, new, out in ((kk_s, ck_ref, kn_ref, nk_ref), (vv_s, cv_ref, vn_ref, nv_ref)):
            stage[base:base + WINDOW, :] = cache[sl]
            stage[base + WINDOW:base + WINDOW + dec_seq, :] = new[tok, :]
            stage[base + WINDOW + dec_seq:base + span, :] = jnp.zeros((WINDOW - dec_seq, KV_WIDTH), F32)
            out[sl] = stage[base + dec_seq:base + dec_seq + WINDOW, :]
        for stage, cache in ((km_s, cmk_ref), (vm_s, cmv_ref)):
            for hm in range(MEM_HEADS):
                stage[sl * MEM_LEN:(sl + 1) * MEM_LEN, hm * MEM_HEAD_DIM:(hm + 1) * MEM_HEAD_DIM] = (
                    cache[sl, pl.ds(hm, MEM_LEN, stride=MEM_HEADS), :].astype(BF16))

    def row_info(nrows):
        r = lax.broadcasted_iota(jnp.int32, (nrows, 1), 0)
        return r // ntok, (r % ntok) // dec_seq, (r % ntok) % dec_seq

    head, seq_r, t_r = row_info(SWA_HEADS * ntok)
    col = lax.broadcasted_iota(jnp.int32, (1, sb * span), 1)
    dist = (WINDOW + t_r) - col % span
    valid = (col // span == seq_r) & (dist >= 0) & (dist <= WINDOW)
    sink = jnp.zeros(head.shape, F32)
    for hq in range(SWA_HEADS):
        sink = jnp.where(head == hq, sinks_ref[hq], sink)
    qs = qs_ref[...].reshape(SWA_HEADS * ntok, KV_WIDTH).astype(BF16)
    s = _dot_nt(qs, kk_s[...].astype(BF16)) * SWA_SCALE - jnp.exp2(-(head + 1).astype(F32)) * dist.astype(F32)
    s = jnp.where(valid, s, NEG)
    m = jnp.maximum(jnp.max(s, axis=-1, keepdims=True), sink)
    e = jnp.exp(s - m)
    w = e / (jnp.sum(e, axis=-1, keepdims=True) + jnp.exp(sink - m))
    vv = vv_s[...]
    vr = pltpu.roll(vv, SWA_HEAD_DIM, 1)
    lane_kv = lax.broadcasted_iota(jnp.int32, vv.shape, 1) // SWA_HEAD_DIM
    dup0 = jnp.where(lane_kv == 0, vv, vr).astype(BF16)
    dup1 = jnp.where(lane_kv == 1, vv, vr).astype(BF16)
    o = _dot(w.astype(BF16), jnp.concatenate([dup0, dup0, dup1, dup1], axis=1))
    lane_s = lax.broadcasted_iota(jnp.int32, (ntok, SWA_WIDTH), 1) // SWA_HEAD_DIM
    acc = jnp.zeros((ntok, SWA_WIDTH), F32)
    for hq in range(SWA_HEADS):
        acc = acc + jnp.where(lane_s == hq, o[hq * ntok:(hq + 1) * ntok, :], 0.0)
    aswa_ref[...] = acc

    _, seq_m, _ = row_info(MEM_HEADS * ntok)
    colm = lax.broadcasted_iota(jnp.int32, (1, sb * MEM_LEN), 1)
    qp = qp_ref[...].reshape(MEM_HEADS * ntok, MEM_WIDTH).astype(BF16)
    sm = jnp.where(colm // MEM_LEN == seq_m, _dot_nt(qp, km_s[...]) * MEM_SCALE, NEG)
    om = _dot(_softmax_rows(sm).astype(BF16), vm_s[...])
    lane_m = lax.broadcasted_iota(jnp.int32, (ntok, MEM_WIDTH), 1) // MEM_HEAD_DIM
    accm = jnp.zeros((ntok, MEM_WIDTH), F32)
    for hm in range(MEM_HEADS):
        accm = accm + jnp.where(lane_m == hm, om[hm * ntok:(hm + 1) * ntok, :], 0.0)
    amem_ref[...] = accm


def _sample_attn_call(sinks, qs, qp, kn, vn, ck, cv, cmk, cmv, *, dec_seq):
    nseq = ck.shape[0]
    sb = SB_SAMPLE
    rows = lambda w: pl.BlockSpec((sb * dec_seq, w), lambda i: (i, 0))
    planes = lambda p, w: pl.BlockSpec((p, sb * dec_seq, w), lambda i: (0, i, 0))
    seqs = lambda r, w: pl.BlockSpec((sb, r, w), lambda i: (i, 0, 0))
    n = nseq * dec_seq
    return pl.pallas_call(
        functools.partial(_sample_attn_kernel, dec_seq=dec_seq),
        grid=(nseq // sb,),
        in_specs=[_smem_spec(), planes(SWA_HEADS, KV_WIDTH), planes(MEM_HEADS, MEM_WIDTH),
                  rows(KV_WIDTH), rows(KV_WIDTH), seqs(WINDOW, KV_WIDTH), seqs(WINDOW, KV_WIDTH),
                  seqs(MEM_LEN * MEM_HEADS, MEM_HEAD_DIM), seqs(MEM_LEN * MEM_HEADS, MEM_HEAD_DIM)],
        out_specs=[rows(SWA_WIDTH), rows(MEM_WIDTH), seqs(WINDOW, KV_WIDTH), seqs(WINDOW, KV_WIDTH)],
        out_shape=[jax.ShapeDtypeStruct((n, SWA_WIDTH), F32), jax.ShapeDtypeStruct((n, MEM_WIDTH), F32),
                   jax.ShapeDtypeStruct((nseq, WINDOW, KV_WIDTH), F32),
                   jax.ShapeDtypeStruct((nseq, WINDOW, KV_WIDTH), F32)],
        scratch_shapes=[pltpu.VMEM((sb * 2 * WINDOW, KV_WIDTH), F32), pltpu.VMEM((sb * 2 * WINDOW, KV_WIDTH), F32),
                        pltpu.VMEM((sb * MEM_LEN, MEM_WIDTH), BF16), pltpu.VMEM((sb * MEM_LEN, MEM_WIDTH), BF16)],
        compiler_params=pltpu.CompilerParams(dimension_semantics=("arbitrary",), vmem_limit_bytes=VMEM_LIMIT),
        name="sample_attn",
    )(sinks, qs, qp, kn, vn, ck, cv, cmk, cmv)


def _sample_merge_kernel(x_ref, aswa_ref, achk_ref, amem_ref, g1_ref, win_ref, bg_ref, wbo_ref, wout_ref, g2_ref,
                         h_ref, xn2_ref):
    x = x_ref[...]
    xn = _rms(x, g1_ref[...]).astype(BF16)
    _merge_tail(x, xn, (aswa_ref[...].astype(BF16), achk_ref[...], amem_ref[...].astype(BF16)),
                win_ref, bg_ref, wbo_ref, wout_ref, g2_ref, h_ref, xn2_ref)


def _sample_merge_call(x2, aswa, achk, amem, g1, w_in_b, bg, wbo_b, wout_b, g2):
    n = x2.shape[0]
    full = lambda w: pl.BlockSpec((n, w), lambda i: (0, 0))
    return pl.pallas_call(
        _sample_merge_kernel,
        grid=(1,),
        in_specs=[full(D_MODEL), full(BRANCH_WIDTH), full(BRANCH_WIDTH), full(BRANCH_WIDTH),
                  _const_spec((1, D_MODEL)), _const_spec((D_MODEL, IN_WIDTH)), _const_spec((N_BRANCH, D_MODEL)),
                  _const_spec((N_BRANCH, BRANCH_WIDTH, D_MODEL)), _const_spec((D_MODEL, D_MODEL)),
                  _const_spec((1, D_MODEL))],
        out_specs=[full(D_MODEL), full(D_MODEL)],
        out_shape=[jax.ShapeDtypeStruct((n, D_MODEL), F32), jax.ShapeDtypeStruct((n, D_MODEL), BF16)],
        compiler_params=pltpu.CompilerParams(dimension_semantics=("arbitrary",), vmem_limit_bytes=VMEM_LIMIT),
        name="sample_merge",
    )(x2, aswa, achk, amem, g1, w_in_b, bg, wbo_b, wout_b, g2)


def _expert_tables_kernel(pu_ref, pv_ref, pub_ref, pvt_ref):
    pub_ref[...] = pu_ref[...].astype(BF16)
    pvt_ref[...] = pv_ref[...].T.astype(BF16)


def _expert_tables_call(peer_u, peer_v):
    _, ne, d = peer_u.shape
    rows = EB_DENSE
    blk = pl.BlockSpec((None, rows, d), lambda i: (0, i, 0))
    return pl.pallas_call(
        _expert_tables_kernel,
        grid=(ne // rows,),
        in_specs=[blk, blk],
        out_specs=[pl.BlockSpec((rows, d), lambda i: (i, 0)), pl.BlockSpec((None, d, rows), lambda i: (i, 0, 0))],
        out_shape=[jax.ShapeDtypeStruct((ne, d), BF16), jax.ShapeDtypeStruct((ne // rows, d, rows), BF16)],
        compiler_params=pltpu.CompilerParams(dimension_semantics=("arbitrary",), vmem_limit_bytes=VMEM_LIMIT),
        name="expert_tables",
    )(peer_u, peer_v)


def _top16(s, key_iota):
    rank = jnp.full(s.shape, float(PEER_TOPK), F32)
    vals = []
    for kth in range(PEER_TOPK):
        m = jnp.max(s, axis=0, keepdims=True)
        idx = jnp.min(jnp.where(s == m, key_iota, float(PEER_N_KEYS)), axis=0, keepdims=True)
        hit = key_iota == idx
        rank = jnp.where(hit, float(kth), rank)
        s = jnp.where(hit, -jnp.inf, s)
        vals.append(m)
    return rank, vals


def _route_exact(s1, s2):
    lanes = s1.shape[1]
    key_iota = lax.broadcasted_iota(jnp.int32, (PEER_N_KEYS, lanes), 0).astype(F32)
    cand_iota = lax.broadcasted_iota(jnp.int32, (PEER_TOPK * PEER_TOPK, lanes), 0).astype(F32)
    k16 = lax.broadcasted_iota(jnp.int32, (PEER_TOPK, lanes), 0)
    rank1, v1 = _top16(s1, key_iota)
    rank2, v2 = _top16(s2, key_iota)
    top2 = jnp.zeros((PEER_TOPK, lanes), F32)
    for b in range(PEER_TOPK):
        top2 = jnp.where(k16 == b, v2[b], top2)
    cand = jnp.concatenate([v1[a] + top2 for a in range(PEER_TOPK)], axis=0)
    sel = jnp.zeros(cand.shape, F32)
    cmax = v1[0] + v2[0]
    z = jnp.zeros((1, lanes), F32)
    for _ in range(PEER_TOPK):
        m = jnp.max(cand, axis=0, keepdims=True)
        idx = jnp.min(jnp.where(cand == m, cand_iota, float(PEER_TOPK * PEER_TOPK)), axis=0, keepdims=True)
        hit = cand_iota == idx
        sel = jnp.where(hit, 1.0, sel)
        cand = jnp.where(hit, -jnp.inf, cand)
        z = z + jnp.exp(m - cmax)
    cnt = jnp.zeros((PEER_N_KEYS, lanes), F32)
    for a in range(PEER_TOPK):
        n_a = jnp.sum(sel[a * PEER_TOPK:(a + 1) * PEER_TOPK], axis=0, keepdims=True)
        cnt = jnp.where(rank1 == float(a), n_a, cnt)
    return cnt, rank2, jnp.exp(s1 - v1[0]), jnp.exp(s2 - v2[0]) * (GELU_HALF / z)


def _hi_lo(a, b):
    if a is None or b is None:
        return (b if a is None else a), None
    return jnp.maximum(a, b), jnp.minimum(a, b)


def _sort16_desc(v):
    n = len(v)
    k = 2
    while k <= n:
        j = k // 2
        while j >= 1:
            for i in range(n):
                l = i ^ j
                if l > i:
                    hi, lo = _hi_lo(v[i], v[l])
                    v[i], v[l] = (hi, lo) if (i & k) == 0 else (lo, hi)
            j //= 2
        k *= 2


def _merge16_desc(v):
    j = len(v) // 2
    while j >= 1:
        for i in range(len(v)):
            l = i ^ j
            if l > i:
                v[i], v[l] = _hi_lo(v[i], v[l])
        j //= 2


def _sublane_allsum(x):
    for shift in (4, 2, 1):
        x = x + pltpu.roll(x, shift, 0)
    return x


def _sorted_top16(tiles):
    v = list(tiles)
    _sort16_desc(v)
    for shift in (4, 2, 1):
        partner = [None if x is None else pltpu.roll(x, shift, 0) for x in reversed(v)]
        v = [_hi_lo(a, b)[0] for a, b in zip(v, partner)]
        _merge16_desc(v)
    return v


def _route_fast(s1, s2):
    nt = PEER_N_KEYS // SUBLANES
    lanes = s1.shape[1]
    sub = lax.broadcasted_iota(jnp.int32, (SUBLANES, lanes), 0)
    ninf = jnp.full((SUBLANES, lanes), -jnp.inf, F32)
    zero = jnp.zeros((SUBLANES, lanes), F32)
    t1 = [s1[r * SUBLANES:(r + 1) * SUBLANES, :] for r in range(nt)]
    t2 = [s2[r * SUBLANES:(r + 1) * SUBLANES, :] for r in range(nt)]
    v1 = _sorted_top16(t1)
    v2 = _sorted_top16(t2)

    rank2 = []
    for x in t2:
        r = jnp.full((SUBLANES, lanes), float(PEER_TOPK), F32)
        for k in range(PEER_TOPK - 1, -1, -1):
            r = jnp.where(x >= v2[k], float(k), r)
        rank2.append(r)

    def rows(v, base):
        out = v[base]
        for b in range(1, SUBLANES):
            out = jnp.where(sub == b, v[base + b], out)
        return out

    r1 = [rows(v1, 0), rows(v1, SUBLANES)]
    r2 = [rows(v2, 0), rows(v2, SUBLANES)]
    cands = [v1[0] + r2[0], v1[0] + r2[1], v1[1] + r2[0],
             jnp.where(sub < 5, v1[2] + r2[0], ninf), jnp.where(sub < 4, v1[3] + r2[0], ninf),
             jnp.where(sub >= 4, v2[0] + r1[0], ninf), v2[0] + r1[1],
             jnp.where(sub >= 4, v2[1] + r1[0], ninf), jnp.where(sub == 4, v2[2] + r1[0], ninf)]
    top = _sorted_top16(cands + [None] * (PEER_TOPK - len(cands)))
    tau, cmax = top[PEER_TOPK - 1], top[0]
    z = zero
    for c in cands:
        z = z + jnp.where(c >= tau, jnp.exp(c - cmax), 0.0)
    z = _sublane_allsum(z)

    cnt = []
    for x in t1:
        c = zero
        for b in range(3):
            c = jnp.where(x + v2[b] >= tau, float(b + 1), c)
        cnt.append(c)
    for a in range(4):
        n_a = zero
        for b in range(PEER_TOPK // (a + 1)):
            n_a = jnp.where(v1[a] + v2[b] >= tau, float(b + 1), n_a)
        cnt = [jnp.where(x == v1[a], n_a, c) for x, c in zip(t1, cnt)]

    total = _sublane_allsum(functools.reduce(lambda a, b: a + b, cnt))
    ranked = _sublane_allsum(functools.reduce(lambda a, b: a + b,
                                              [jnp.where(r < float(PEER_TOPK), 1.0, 0.0) for r in rank2]))
    bad = jnp.where(total != float(PEER_TOPK), 1.0, 0.0) + jnp.where(ranked != float(PEER_TOPK), 1.0, 0.0)
    for k in range(PEER_TOPK - 1):
        bad = bad + jnp.where(v2[k] == v2[k + 1], 1.0, 0.0)
    for k in range(4):
        bad = bad + jnp.where(v1[k] == v1[k + 1], 1.0, 0.0)

    rz = GELU_HALF / z
    cat = lambda tiles: jnp.concatenate(tiles, axis=0)
    return (cat(cnt), cat(rank2), cat([jnp.exp(x - v1[0]) for x in t1]),
            cat([jnp.exp(x - v2[0]) * rz for x in t2]), bad)


def _peer_route_kernel(xn2_ref, wqt_ref, sub_ref, cnt_ref, rnk_ref, c1_ref, e2_ref, sc_s):
    tp = xn2_ref.shape[0]
    lanes = LANES

    qt = _dot_nt(wqt_ref[...], xn2_ref[...]).astype(BF16)
    for hp in range(2 * PEER_HEADS):
        sc_s[hp] = _dot(sub_ref[hp % 2], qt[hp * PEER_HALF:(hp + 1) * PEER_HALF, :])

    def head_body(h, carry):
        for lg in range(tp // lanes):
            cols = slice(lg * lanes, (lg + 1) * lanes)
            s1 = sc_s[h * 2, :, cols]
            s2 = sc_s[h * 2 + 1, :, cols]

            def store(cnt, rank2, c1, e2):
                cnt_ref[h, :, cols] = cnt
                rnk_ref[h, :, cols] = rank2.astype(BF16)
                c1_ref[h, :, cols] = c1
                e2_ref[h, :, cols] = e2.astype(BF16)

            *tables, bad = _route_fast(s1, s2)
            store(*tables)

            @pl.when(jnp.max(bad) > 0.0)
            def _():
                store(*_route_exact(s1, s2))
        return carry

    lax.fori_loop(0, PEER_HEADS, head_body, 0)


def _peer_route_call(xn2, wqt_b, sub_b):
    n = xn2.shape[0]
    tp = TP_PRE
    tab = pl.BlockSpec((PEER_HEADS, PEER_N_KEYS, tp), lambda i: (0, 0, i))
    return pl.pallas_call(
        _peer_route_kernel,
        grid=(n // tp,),
        in_specs=[pl.BlockSpec((tp, D_MODEL), lambda i: (i, 0)),
                  _const_spec((PEER_HEADS * 2 * PEER_HALF, D_MODEL)), _const_spec((2, PEER_N_KEYS, PEER_HALF))],
        out_specs=[tab] * 4,
        out_shape=[jax.ShapeDtypeStruct((PEER_HEADS, PEER_N_KEYS, n), dt) for dt in (F32, BF16, F32, BF16)],
        scratch_shapes=[pltpu.VMEM((2 * PEER_HEADS, PEER_N_KEYS, tp), F32)],
        compiler_params=pltpu.CompilerParams(dimension_semantics=("arbitrary",), vmem_limit_bytes=VMEM_LIMIT),
        name="peer_route",
    )(xn2, wqt_b, sub_b)


def _peer_dense_kernel(xn2_ref, pu_ref, pvt_ref, cnt_ref, c1_ref, rnk_ref, e2_ref, h_ref, gf_ref,
                       y_ref, acc_ref, at_s, g_s, rnk_s, e2_s, xt_s):
    k = pl.program_id(1)
    kps = 4
    eb = kps * PEER_N_KEYS
    tm = xn2_ref.shape[0]
    sub_rows = BF16_TILE_ROWS
    ntile = PEER_N_KEYS // sub_rows

    @pl.when(k == 0)
    def _():
        acc_ref[...] = jnp.zeros_like(acc_ref)
        rnk_s[...] = rnk_ref[...]
        e2_s[...] = e2_ref[...]
        xt_s[...] = xn2_ref[...].astype(F32).T.astype(BF16)

    def weights_and_gate(row0, base):
        cnt_rows = [[cnt_ref[h, row0 + ii:row0 + ii + 1, :].astype(BF16) for ii in range(kps)]
                    for h in range(PEER_HEADS)]
        c1_rows = [[c1_ref[h, row0 + ii:row0 + ii + 1, :].astype(BF16) for ii in range(kps)]
                   for h in range(PEER_HEADS)]
        for cg in range(tm // LANES):
            cols = slice(cg * LANES, (cg + 1) * LANES)
            w = [[None] * ntile for _ in range(kps)]
            for h in range(PEER_HEADS):
                cnt_b = [jnp.broadcast_to(cnt_rows[h][ii][:, cols], (sub_rows, LANES)) for ii in range(kps)]
                c1_b = [jnp.broadcast_to(c1_rows[h][ii][:, cols], (sub_rows, LANES)) for ii in range(kps)]
                for r in range(ntile):
                    jr = slice(r * sub_rows, (r + 1) * sub_rows)
                    rt = rnk_s[h, jr, cols]
                    et = e2_s[h, jr, cols]
                    for ii in range(kps):
                        term = jnp.where(rt < cnt_b[ii], et, 0.0) * c1_b[ii]
                        w[ii][r] = term if w[ii][r] is None else w[ii][r] + term
            for ii in range(kps):
                for r in range(ntile):
                    e0 = base + ii * PEER_N_KEYS + r * sub_rows
                    er = slice(e0, e0 + sub_rows)
                    a = at_s[er, cols]
                    g_s[er, cols] = (a * (1.0 + lax.erf(a * (2.0 ** -0.5)))).astype(BF16) * w[ii][r]

    at_s[...] = _dot(pu_ref[...], xt_s[...])
    for b in range(at_s.shape[0] // eb):
        weights_and_gate(b * kps, b * eb)
    acc_ref[...] += _dot(pvt_ref[...], g_s[...])

    @pl.when(k == pl.num_programs(1) - 1)
    def _():
        y_ref[...] = _rms(h_ref[...] + acc_ref[...].T, gf_ref[...])


def _peer_dense_call(xn2, pu_b, pvt_b, cnt, rnk, c1, e2, h, gf):
    n = xn2.shape[0]
    tm, eb = min(TM_DENSE, n), EB_DENSE
    nstep = PEER_N_EXPERTS // eb
    tok = lambda dt: pl.BlockSpec((tm, D_MODEL), lambda t, k: (t, 0))
    tab = pl.BlockSpec((PEER_HEADS, PEER_N_KEYS, tm), lambda t, k: (0, 0, t))
    rows = pl.BlockSpec((PEER_HEADS, eb // PEER_N_KEYS, tm), lambda t, k: (0, k, t))
    return pl.pallas_call(
        _peer_dense_kernel,
        grid=(n // tm, nstep),
        in_specs=[tok(BF16), pl.BlockSpec((eb, D_MODEL), lambda t, k: (k, 0)),
                  pl.BlockSpec((None, D_MODEL, eb), lambda t, k: (k, 0, 0)),
                  rows, rows, tab, tab, tok(F32), pl.BlockSpec((1, D_MODEL), lambda t, k: (0, 0))],
        out_specs=tok(F32),
        out_shape=jax.ShapeDtypeStruct((n, D_MODEL), F32),
        scratch_shapes=[pltpu.VMEM((D_MODEL, tm), F32), pltpu.VMEM((eb, tm), F32), pltpu.VMEM((eb, tm), BF16),
                        pltpu.VMEM((PEER_HEADS, PEER_N_KEYS, tm), BF16),
                        pltpu.VMEM((PEER_HEADS, PEER_N_KEYS, tm), BF16),
                        pltpu.VMEM((D_MODEL, tm), BF16)],
        compiler_params=pltpu.CompilerParams(dimension_semantics=("arbitrary", "arbitrary"),
                                             vmem_limit_bytes=VMEM_LIMIT),
        name="peer_dense",
    )(xn2, pu_b, pvt_b, cnt, c1, rnk, e2, h, gf)


def _peer_and_final_norm(h, xn2, wqt_b, sub_b, pu_b, pvt_b, gf):
    cnt, rnk, c1, e2 = _peer_route_call(xn2, wqt_b, sub_b)
    return _peer_dense_call(xn2, pu_b, pvt_b, cnt, rnk, c1, e2, h, gf)


def kernel(x_prompt, x_sample, cache_swa_k, cache_swa_v, cache_mem_k, cache_mem_v, mem_prompt,
           norm1_g, w_in, b_gate, swa_sinks, chunk_norm_g, chunk_ws, chunk_bs, mem_norm_g, w_mem_kv,
           w_branch_out, w_out, norm2_g, peer_wq, peer_subkeys, peer_u, peer_v, final_norm_g):
    depth = norm1_g.shape[0]
    assert depth == 1, "single-layer step"
    batch, seq, _ = x_prompt.shape
    nseq, dec_seq, _ = x_sample.shape

    g1 = norm1_g[0][None]
    g2 = norm2_g[0][None]
    gf = final_norm_g[None]
    cng = chunk_norm_g[0][None]
    w_in_b = w_in[0].astype(BF16)
    wbo_b = w_branch_out[0].astype(BF16)
    wout_b = w_out[0].astype(BF16)
    wqt_b = peer_wq[0].T.astype(BF16)
    sub_b = peer_subkeys[0].astype(BF16)
    pu_b, pvt_b = _expert_tables_call(peer_u, peer_v)
    sinks = swa_sinks[0]
    bst = chunk_bs[0].T
    ws4 = chunk_ws[0][:, :dec_seq, :dec_seq].reshape(-1)
    bs4 = chunk_bs[0][:, :dec_seq].reshape(-1)

    mk, mv, mkb, mvb = _memkv_call(mem_prompt, mem_norm_g[0][None], w_mem_kv[0].astype(BF16))
    hp, xn2p, kp, vp = _prompt_layer_call(
        x_prompt.reshape(batch * seq, D_MODEL), mkb, mvb, sinks, g1, w_in_b, cng, chunk_ws[0], bst,
        b_gate[0], wbo_b, wout_b, g2, batch=batch, seq=seq)
    y_prompt = _peer_and_final_norm(hp, xn2p, wqt_b, sub_b, pu_b, pvt_b, gf).reshape(batch, seq, D_MODEL)

    xs = x_sample.reshape(nseq * dec_seq, D_MODEL)
    qs, qp, kn, vn, vcs, achk = _sample_proj_call(xs, ws4, bs4, g1, w_in_b, cng, dec_seq=dec_seq)
    aswa, amem, nk, nv = _sample_attn_call(
        sinks, qs, qp, kn, vn,
        cache_swa_k.reshape(nseq, WINDOW, KV_WIDTH), cache_swa_v.reshape(nseq, WINDOW, KV_WIDTH),
        cache_mem_k.reshape(nseq, MEM_LEN * MEM_HEADS, MEM_HEAD_DIM),
        cache_mem_v.reshape(nseq, MEM_LEN * MEM_HEADS, MEM_HEAD_DIM),
        dec_seq=dec_seq)
    hs, xn2s = _sample_merge_call(xs, aswa, achk, amem, g1, w_in_b, b_gate[0], wbo_b, wout_b, g2)
    y_sample = _peer_and_final_norm(hs, xn2s, wqt_b, sub_b, pu_b, pvt_b, gf).reshape(nseq, dec_seq, D_MODEL)

    kv5 = lambda a, b: a.reshape(1, b, WINDOW, SWA_KV_HEADS, SWA_HEAD_DIM)
    mem5 = lambda a: a.reshape(1, batch, MEM_LEN, MEM_HEADS, MEM_HEAD_DIM)
    return (y_prompt, y_sample, kv5(kp, batch), kv5(vp, batch), mem5(mk), mem5(mv),
            kv5(nk, nseq), kv5(nv, nseq), vcs.reshape(1, nseq, dec_seq, CHUNK_GROUPS, CHUNK_GROUP_DIM))
```

```python
import functools

import jax
import jax.numpy as jnp
from jax import lax
from jax.experimental import pallas as pl
from jax.experimental.pallas import tpu as pltpu

F32 = jnp.float32
BF16 = jnp.bfloat16

D_MODEL = 1024
SWA_HEADS = 8
SWA_KV_HEADS = 2
SWA_GROUP = SWA_HEADS // SWA_KV_HEADS
SWA_HEAD_DIM = 64
SWA_WIDTH = SWA_HEADS * SWA_HEAD_DIM
KV_WIDTH = SWA_KV_HEADS * SWA_HEAD_DIM
WINDOW = 128
SWA_SCALE = SWA_HEAD_DIM ** -0.5
CHUNK = 128
CHUNK_GROUPS = 4
CHUNK_GROUP_DIM = 128
CHUNK_WIDTH = CHUNK_GROUPS * CHUNK_GROUP_DIM
MEM_LEN = 256
MEM_HEADS = 4
MEM_HEAD_DIM = 128
MEM_WIDTH = MEM_HEADS * MEM_HEAD_DIM
MEM_SCALE = MEM_HEAD_DIM ** -0.5
N_BRANCH = 3
BRANCH_WIDTH = 512
PEER_HEADS = 8
PEER_N_KEYS = 128
PEER_N_EXPERTS = PEER_N_KEYS * PEER_N_KEYS
PEER_HALF = 128
PEER_TOPK = 16
EPS = 1e-6
NEG = -1e30

O_Q = 0
O_K = O_Q + SWA_WIDTH
O_V = O_K + KV_WIDTH
O_UV = O_V + KV_WIDTH
O_QM = O_UV + 2 * CHUNK_WIDTH
O_GL = O_QM + MEM_WIDTH
IN_WIDTH = O_GL + N_BRANCH * D_MODEL

LANES = 128
SUBLANES = 8
BF16_TILE_ROWS = 16

TM_LAYER = 512
TP_PRE = 512
TM_DENSE = 512
EB_DENSE = 2048
SB_SAMPLE = 8
VMEM_LIMIT = 56 * 1024 * 1024

_NT = (((1,), (1,)), ((), ()))
GELU_HALF = 0.5


def _rms(x, g):
    return x * lax.rsqrt(jnp.mean(x * x, axis=-1, keepdims=True) + EPS) * g


def _gelu(x):
    return 0.5 * x * (1.0 + lax.erf(x * (2.0 ** -0.5)))


def _dot(a, b):
    return jnp.dot(a, b, preferred_element_type=F32)


def _dot_nt(a, b):
    return lax.dot_general(a, b, _NT, preferred_element_type=F32)


def _const_spec(shape):
    nd = len(shape)
    return pl.BlockSpec(shape, lambda *_: (0,) * nd, pipeline_mode=pl.Buffered(1))


def _smem_spec():
    return pl.BlockSpec(memory_space=pltpu.SMEM)


def _chunk_uv(xn, win_ref, cng_ref):
    uv = _gelu(_dot(xn, win_ref[:, O_UV:O_QM]))
    return uv[:, :CHUNK_WIDTH], _rms(uv[:, CHUNK_WIDTH:], cng_ref[...])


def _merge_tail(x, xn, branches, win_ref, bg_ref, wbo_ref, wout_ref, g2_ref, h_ref, xn2_ref):
    acc = None
    for j in range(N_BRANCH):
        pj = _dot(branches[j], wbo_ref[j])
        gl = _dot(xn, win_ref[:, O_GL + j * D_MODEL:O_GL + (j + 1) * D_MODEL]) + bg_ref[j:j + 1, :]
        term = jax.nn.sigmoid(gl) * pj
        acc = term if acc is None else acc + term
    h = x + _dot(acc.astype(BF16), wout_ref[...])
    h_ref[...] = h
    xn2_ref[...] = _rms(h, g2_ref[...]).astype(BF16)


def _softmax_rows(s):
    m = jnp.max(s, axis=-1, keepdims=True)
    e = jnp.exp(s - m)
    return e / jnp.sum(e, axis=-1, keepdims=True)


def _memkv_kernel(mem_ref, g_ref, w_ref, k_ref, v_ref, kb_ref, vb_ref):
    kv = _dot(_rms(mem_ref[...], g_ref[...]).astype(BF16), w_ref[...])
    k = kv[:, :MEM_WIDTH]
    v = kv[:, MEM_WIDTH:]
    k_ref[...] = k
    v_ref[...] = v
    kb_ref[...] = k.astype(BF16)
    vb_ref[...] = v.astype(BF16)


def _memkv_call(mem, g, w_b):
    b, m, d = mem.shape
    blk = lambda w: pl.BlockSpec((None, m, w), lambda i: (i, 0, 0))
    return pl.pallas_call(
        _memkv_kernel,
        grid=(b,),
        in_specs=[blk(d), _const_spec((1, d)), _const_spec((d, 2 * MEM_WIDTH))],
        out_specs=[blk(MEM_WIDTH)] * 4,
        out_shape=[jax.ShapeDtypeStruct((b, m, MEM_WIDTH), F32)] * 2
        + [jax.ShapeDtypeStruct((b, m, MEM_WIDTH), BF16)] * 2,
        compiler_params=pltpu.CompilerParams(dimension_semantics=("arbitrary",), vmem_limit_bytes=VMEM_LIMIT),
        name="mem_kv",
    )(mem, g, w_b)


def _prompt_layer_kernel(sinks_ref, x_ref, mk_ref, mv_ref, g1_ref, win_ref, cng_ref, ws_ref, bst_ref,
                         bg_ref, wbo_ref, wout_ref, g2_ref,
                         h_ref, xn2_ref, kout_ref, vout_ref,
                         kprev_ref, vprev_ref, aswa_ref, achk_ref, amem_ref, *, tiles_per_batch):
    tm = x_ref.shape[0]
    nblk = tm // WINDOW
    i = pl.program_id(0)
    first = (i % tiles_per_batch) == 0

    @pl.when(first)
    def _():
        kprev_ref[...] = jnp.zeros_like(kprev_ref)
        vprev_ref[...] = jnp.zeros_like(vprev_ref)

    x = x_ref[...]
    xn = _rms(x, g1_ref[...]).astype(BF16)

    q = _dot(xn, win_ref[:, O_Q:O_K]).astype(BF16)
    k = _dot(xn, win_ref[:, O_K:O_V])
    v = _dot(xn, win_ref[:, O_V:O_UV])
    kout_ref[...] = k[tm - WINDOW:]
    vout_ref[...] = v[tm - WINDOW:]
    kb = k.astype(BF16)
    vb = v.astype(BF16)
    r = lax.broadcasted_iota(jnp.int32, (WINDOW, 2 * WINDOW), 0)
    c = lax.broadcasted_iota(jnp.int32, (WINDOW, 2 * WINDOW), 1)
    dist = WINDOW + r - c
    band = (dist >= 0) & (dist <= WINDOW)
    distf = dist.astype(F32)
    first_lo = jnp.where(first, WINDOW, 0)
    for n in range(nblk):
        rows = slice(n * WINDOW, (n + 1) * WINDOW)
        if n == 0:
            kp, vp = kprev_ref[...], vprev_ref[...]
            valid = band & (c >= first_lo)
        else:
            prev = slice((n - 1) * WINDOW, n * WINDOW)
            kp, vp = kb[prev], vb[prev]
            valid = band
        kk = jnp.concatenate([kp, kb[rows]], axis=0)
        vv = jnp.concatenate([vp, vb[rows]], axis=0)
        for hq in range(SWA_HEADS):
            kvs = slice((hq // SWA_GROUP) * SWA_HEAD_DIM, (hq // SWA_GROUP + 1) * SWA_HEAD_DIM)
            hs = slice(hq * SWA_HEAD_DIM, (hq + 1) * SWA_HEAD_DIM)
            s = _dot_nt(q[rows, hs], kk[:, kvs]) * SWA_SCALE - (2.0 ** -(hq + 1)) * distf
            s = jnp.where(valid, s, NEG)
            sink = sinks_ref[hq]
            m = jnp.maximum(jnp.max(s, axis=-1, keepdims=True), sink)
            e = jnp.exp(s - m)
            w = e / (jnp.sum(e, axis=-1, keepdims=True) + jnp.exp(sink - m))
            aswa_ref[rows, hs] = _dot(w.astype(BF16), vv[:, kvs]).astype(BF16)
    kprev_ref[...] = kb[tm - WINDOW:]
    vprev_ref[...] = vb[tm - WINDOW:]

    u, vc = _chunk_uv(xn, win_ref, cng_ref)
    vcb = vc.astype(BF16)
    tri = (lax.broadcasted_iota(jnp.int32, (CHUNK, CHUNK), 0)
           >= lax.broadcasted_iota(jnp.int32, (CHUNK, CHUNK), 1))
    for g in range(CHUNK_GROUPS):
        gs = slice(g * CHUNK_GROUP_DIM, (g + 1) * CHUNK_GROUP_DIM)
        wg = jnp.where(tri, ws_ref[g], 0.0).astype(BF16)
        for n in range(nblk):
            rows = slice(n * CHUNK, (n + 1) * CHUNK)
            sg = _dot(wg, vcb[rows, gs]) + bst_ref[:, g:g + 1]
            achk_ref[rows, gs] = (u[rows, gs] * sg).astype(BF16)

    qm = _dot(xn, win_ref[:, O_QM:O_GL]).astype(BF16)
    for hm in range(MEM_HEADS):
        hs = slice(hm * MEM_HEAD_DIM, (hm + 1) * MEM_HEAD_DIM)
        w = _softmax_rows(_dot_nt(qm[:, hs], mk_ref[:, hs]) * MEM_SCALE)
        amem_ref[:, hs] = _dot(w.astype(BF16), mv_ref[:, hs]).astype(BF16)

    _merge_tail(x, xn, (aswa_ref[...], achk_ref[...], amem_ref[...]),
                win_ref, bg_ref, wbo_ref, wout_ref, g2_ref, h_ref, xn2_ref)


def _prompt_layer_call(x2, mkb, mvb, sinks, g1, w_in_b, cng, ws, bst, bg, wbo_b, wout_b, g2, *, batch, seq):
    n = x2.shape[0]
    tm = TM_LAYER
    tpb = seq // tm
    tile = lambda w: pl.BlockSpec((tm, w), lambda i: (i, 0))
    per_batch = lambda r, w: pl.BlockSpec((None, r, w), lambda i: (i // tpb, 0, 0))
    return pl.pallas_call(
        functools.partial(_prompt_layer_kernel, tiles_per_batch=tpb),
        grid=(n // tm,),
        in_specs=[_smem_spec(), tile(D_MODEL), per_batch(MEM_LEN, MEM_WIDTH), per_batch(MEM_LEN, MEM_WIDTH),
                  _const_spec((1, D_MODEL)), _const_spec((D_MODEL, IN_WIDTH)), _const_spec((1, CHUNK_WIDTH)),
                  _const_spec((CHUNK_GROUPS, CHUNK, CHUNK)), _const_spec((CHUNK, CHUNK_GROUPS)),
                  _const_spec((N_BRANCH, D_MODEL)), _const_spec((N_BRANCH, BRANCH_WIDTH, D_MODEL)),
                  _const_spec((D_MODEL, D_MODEL)), _const_spec((1, D_MODEL))],
        out_specs=[tile(D_MODEL), tile(D_MODEL), per_batch(WINDOW, KV_WIDTH), per_batch(WINDOW, KV_WIDTH)],
        out_shape=[jax.ShapeDtypeStruct((n, D_MODEL), F32), jax.ShapeDtypeStruct((n, D_MODEL), BF16),
                   jax.ShapeDtypeStruct((batch, WINDOW, KV_WIDTH), F32),
                   jax.ShapeDtypeStruct((batch, WINDOW, KV_WIDTH), F32)],
        scratch_shapes=[pltpu.VMEM((WINDOW, KV_WIDTH), BF16), pltpu.VMEM((WINDOW, KV_WIDTH), BF16),
                        pltpu.VMEM((tm, BRANCH_WIDTH), BF16), pltpu.VMEM((tm, BRANCH_WIDTH), BF16),
                        pltpu.VMEM((tm, BRANCH_WIDTH), BF16)],
        compiler_params=pltpu.CompilerParams(dimension_semantics=("arbitrary",), vmem_limit_bytes=VMEM_LIMIT),
        name="prompt_layer",
    )(sinks, x2, mkb, mvb, g1, w_in_b, cng, ws, bst, bg, wbo_b, wout_b, g2)


def _sample_proj_kernel(ws4_ref, bs4_ref, x_ref, g1_ref, win_ref, cng_ref,
                        qs_ref, qp_ref, k_ref, v_ref, vc_ref, achk_ref, *, dec_seq):
    n = x_ref.shape[0]
    xn = _rms(x_ref[...], g1_ref[...]).astype(BF16)
    q = _dot(xn, win_ref[:, O_Q:O_K])
    k_ref[...] = _dot(xn, win_ref[:, O_K:O_V])
    v_ref[...] = _dot(xn, win_ref[:, O_V:O_UV])
    qm = _dot(xn, win_ref[:, O_QM:O_GL])

    lane = lax.broadcasted_iota(jnp.int32, (n, 2 * SWA_HEAD_DIM), 1)
    for hq in range(SWA_HEADS):
        kvh = hq // SWA_GROUP
        win = q[:, (hq // 2) * LANES:(hq // 2 + 1) * LANES]
        if (hq % 2) != kvh:
            win = pltpu.roll(win, SWA_HEAD_DIM, 1)
        qs_ref[hq] = jnp.where((lane // SWA_HEAD_DIM) == kvh, win, 0.0)
    lane = lax.broadcasted_iota(jnp.int32, (n, MEM_WIDTH), 1)
    for hm in range(MEM_HEADS):
        qp_ref[hm] = jnp.where((lane // MEM_HEAD_DIM) == hm, qm, 0.0)

    u, vc = _chunk_uv(xn, win_ref, cng_ref)
    vc_ref[...] = vc
    tpos = lax.broadcasted_iota(jnp.int32, (n, CHUNK_GROUP_DIM), 0) % dec_seq
    for g in range(CHUNK_GROUPS):
        gs = slice(g * CHUNK_GROUP_DIM, (g + 1) * CHUNK_GROUP_DIM)
        vg = vc[:, gs]
        sg = jnp.zeros((n, CHUNK_GROUP_DIM), F32)
        for t in range(dec_seq):
            sg = jnp.where(tpos == t, bs4_ref[g * dec_seq + t], sg)
        for d in range(dec_seq):
            shifted = vg if d == 0 else pltpu.roll(vg, d, 0)
            coef = jnp.zeros((n, CHUNK_GROUP_DIM), F32)
            for t in range(d, dec_seq):
                coef = jnp.where(tpos == t, ws4_ref[(g * dec_seq + t) * dec_seq + (t - d)], coef)
            sg = sg + coef * shifted
        achk_ref[:, gs] = (u[:, gs] * sg).astype(BF16)


def _sample_proj_call(x2, ws4, bs4, g1, w_in_b, cng, *, dec_seq):
    n = x2.shape[0]
    full = lambda r, w: pl.BlockSpec((r, w), lambda i: (0, 0))
    planes = lambda p, w: pl.BlockSpec((p, n, w), lambda i: (0, 0, 0))
    return pl.pallas_call(
        functools.partial(_sample_proj_kernel, dec_seq=dec_seq),
        grid=(1,),
        in_specs=[_smem_spec(), _smem_spec(), full(n, D_MODEL), _const_spec((1, D_MODEL)),
                  _const_spec((D_MODEL, IN_WIDTH)), _const_spec((1, CHUNK_WIDTH))],
        out_specs=[planes(SWA_HEADS, KV_WIDTH), planes(MEM_HEADS, MEM_WIDTH),
                   full(n, KV_WIDTH), full(n, KV_WIDTH), full(n, CHUNK_WIDTH), full(n, CHUNK_WIDTH)],
        out_shape=[jax.ShapeDtypeStruct((SWA_HEADS, n, KV_WIDTH), F32),
                   jax.ShapeDtypeStruct((MEM_HEADS, n, MEM_WIDTH), F32),
                   jax.ShapeDtypeStruct((n, KV_WIDTH), F32), jax.ShapeDtypeStruct((n, KV_WIDTH), F32),
                   jax.ShapeDtypeStruct((n, CHUNK_WIDTH), F32), jax.ShapeDtypeStruct((n, CHUNK_WIDTH), BF16)],
        compiler_params=pltpu.CompilerParams(dimension_semantics=("arbitrary",), vmem_limit_bytes=VMEM_LIMIT),
        name="sample_proj",
    )(ws4, bs4, x2, g1, w_in_b, cng)


def _sample_attn_kernel(sinks_ref, qs_ref, qp_ref, kn_ref, vn_ref, ck_ref, cv_ref, cmk_ref, cmv_ref,
                        aswa_ref, amem_ref, nk_ref, nv_ref, kk_s, vv_s, km_s, vm_s, *, dec_seq):
    sb = ck_ref.shape[0]
    ntok = sb * dec_seq
    span = 2 * WINDOW
    for sl in range(sb):
        tok = slice(sl * dec_seq, (sl + 1) * dec_seq)
        base = sl * span
        for stage, cache, new, out in ((kk_s, ck_ref, kn_ref, nk_ref), (vv_s, cv_ref, vn_ref, nv_ref)):
            stage[base:base + WINDOW, :] = cache[sl]
            stage[base + WINDOW:base + WINDOW + dec_seq, :] = new[tok, :]
            stage[base + WINDOW + dec_seq:base + span, :] = jnp.zeros((WINDOW - dec_seq, KV_WIDTH), F32)
            out[sl] = stage[base + dec_seq:base + dec_seq + WINDOW, :]
        for stage, cache in ((km_s, cmk_ref), (vm_s, cmv_ref)):
            for hm in range(MEM_HEADS):
                stage[sl * MEM_LEN:(sl + 1) * MEM_LEN, hm * MEM_HEAD_DIM:(hm + 1) * MEM_HEAD_DIM] = (
                    cache[sl, pl.ds(hm, MEM_LEN, stride=MEM_HEADS), :].astype(BF16))

    def row_info(nrows):
        r = lax.broadcasted_iota(jnp.int32, (nrows, 1), 0)
        return r // ntok, (r % ntok) // dec_seq, (r % ntok) % dec_seq

    head, seq_r, t_r = row_info(SWA_HEADS * ntok)
    col = lax.broadcasted_iota(jnp.int32, (1, sb * span), 1)
    dist = (WINDOW + t_r) - col % span
    valid = (col // span == seq_r) & (dist >= 0) & (dist <= WINDOW)
    sink = jnp.zeros(head.shape, F32)
    for hq in range(SWA_HEADS):
        sink = jnp.where(head == hq, sinks_ref[hq], sink)
    qs = qs_ref[...].reshape(SWA_HEADS * ntok, KV_WIDTH).astype(BF16)
    s = _dot_nt(qs, kk_s[...].astype(BF16)) * SWA_SCALE - jnp.exp2(-(head + 1).astype(F32)) * dist.astype(F32)
    s = jnp.where(valid, s, NEG)
    m = jnp.maximum(jnp.max(s, axis=-1, keepdims=True), sink)
    e = jnp.exp(s - m)
    w = e / (jnp.sum(e, axis=-1, keepdims=True) + jnp.exp(sink - m))
    vv = vv_s[...]
    vr = pltpu.roll(vv, SWA_HEAD_DIM, 1)
    lane_kv = lax.broadcasted_iota(jnp.int32, vv.shape, 1) // SWA_HEAD_DIM
    dup0 = jnp.where(lane_kv == 0, vv, vr).astype(BF16)
    dup1 = jnp.where(lane_kv == 1, vv, vr).astype(BF16)
    o = _dot(w.astype(BF16), jnp.concatenate([dup0, dup0, dup1, dup1], axis=1))
    lane_s = lax.broadcasted_iota(jnp.int32, (ntok, SWA_WIDTH), 1) // SWA_HEAD_DIM
    acc = jnp.zeros((ntok, SWA_WIDTH), F32)
    for hq in range(SWA_HEADS):
        acc = acc + jnp.where(lane_s == hq, o[hq * ntok:(hq + 1) * ntok, :], 0.0)
    aswa_ref[...] = acc

    _, seq_m, _ = row_info(MEM_HEADS * ntok)
    colm = lax.broadcasted_iota(jnp.int32, (1, sb * MEM_LEN), 1)
    qp = qp_ref[...].reshape(MEM_HEADS * ntok, MEM_WIDTH).astype(BF16)
    sm = jnp.where(colm // MEM_LEN == seq_m, _dot_nt(qp, km_s[...]) * MEM_SCALE, NEG)
    om = _dot(_softmax_rows(sm).astype(BF16), vm_s[...])
    lane_m = lax.broadcasted_iota(jnp.int32, (ntok, MEM_WIDTH), 1) // MEM_HEAD_DIM
    accm = jnp.zeros((ntok, MEM_WIDTH), F32)
    for hm in range(MEM_HEADS):
        accm = accm + jnp.where(lane_m == hm, om[hm * ntok:(hm + 1) * ntok, :], 0.0)
    amem_ref[...] = accm


def _sample_attn_call(sinks, qs, qp, kn, vn, ck, cv, cmk, cmv, *, dec_seq):
    nseq = ck.shape[0]
    sb = SB_SAMPLE
    rows = lambda w: pl.BlockSpec((sb * dec_seq, w), lambda i: (i, 0))
    planes = lambda p, w: pl.BlockSpec((p, sb * dec_seq, w), lambda i: (0, i, 0))
    seqs = lambda r, w: pl.BlockSpec((sb, r, w), lambda i: (i, 0, 0))
    n = nseq * dec_seq
    return pl.pallas_call(
        functools.partial(_sample_attn_kernel, dec_seq=dec_seq),
        grid=(nseq // sb,),
        in_specs=[_smem_spec(), planes(SWA_HEADS, KV_WIDTH), planes(MEM_HEADS, MEM_WIDTH),
                  rows(KV_WIDTH), rows(KV_WIDTH), seqs(WINDOW, KV_WIDTH), seqs(WINDOW, KV_WIDTH),
                  seqs(MEM_LEN * MEM_HEADS, MEM_HEAD_DIM), seqs(MEM_LEN * MEM_HEADS, MEM_HEAD_DIM)],
        out_specs=[rows(SWA_WIDTH), rows(MEM_WIDTH), seqs(WINDOW, KV_WIDTH), seqs(WINDOW, KV_WIDTH)],
        out_shape=[jax.ShapeDtypeStruct((n, SWA_WIDTH), F32), jax.ShapeDtypeStruct((n, MEM_WIDTH), F32),
                   jax.ShapeDtypeStruct((nseq, WINDOW, KV_WIDTH), F32),
                   jax.ShapeDtypeStruct((nseq, WINDOW, KV_WIDTH), F32)],
        scratch_shapes=[pltpu.VMEM((sb * 2 * WINDOW, KV_WIDTH), F32), pltpu.VMEM((sb * 2 * WINDOW, KV_WIDTH), F32),
                        pltpu.VMEM((sb * MEM_LEN, MEM_WIDTH), BF16), pltpu.VMEM((sb * MEM_LEN, MEM_WIDTH), BF16)],
        compiler_params=pltpu.CompilerParams(dimension_semantics=("arbitrary",), vmem_limit_bytes=VMEM_LIMIT),
        name="sample_attn",
    )(sinks, qs, qp, kn, vn, ck, cv, cmk, cmv)


def _sample_merge_kernel(x_ref, aswa_ref, achk_ref, amem_ref, g1_ref, win_ref, bg_ref, wbo_ref, wout_ref, g2_ref,
                         h_ref, xn2_ref):
    x = x_ref[...]
    xn = _rms(x, g1_ref[...]).astype(BF16)
    _merge_tail(x, xn, (aswa_ref[...].astype(BF16), achk_ref[...], amem_ref[...].astype(BF16)),
                win_ref, bg_ref, wbo_ref, wout_ref, g2_ref, h_ref, xn2_ref)


def _sample_merge_call(x2, aswa, achk, amem, g1, w_in_b, bg, wbo_b, wout_b, g2):
    n = x2.shape[0]
    full = lambda w: pl.BlockSpec((n, w), lambda i: (0, 0))
    return pl.pallas_call(
        _sample_merge_kernel,
        grid=(1,),
        in_specs=[full(D_MODEL), full(BRANCH_WIDTH), full(BRANCH_WIDTH), full(BRANCH_WIDTH),
                  _const_spec((1, D_MODEL)), _const_spec((D_MODEL, IN_WIDTH)), _const_spec((N_BRANCH, D_MODEL)),
                  _const_spec((N_BRANCH, BRANCH_WIDTH, D_MODEL)), _const_spec((D_MODEL, D_MODEL)),
                  _const_spec((1, D_MODEL))],
        out_specs=[full(D_MODEL), full(D_MODEL)],
        out_shape=[jax.ShapeDtypeStruct((n, D_MODEL), F32), jax.ShapeDtypeStruct((n, D_MODEL), BF16)],
        compiler_params=pltpu.CompilerParams(dimension_semantics=("arbitrary",), vmem_limit_bytes=VMEM_LIMIT),
        name="sample_merge",
    )(x2, aswa, achk, amem, g1, w_in_b, bg, wbo_b, wout_b, g2)


def _expert_tables_kernel(pu_ref, pv_ref, pub_ref, pvt_ref):
    pub_ref[...] = pu_ref[...].astype(BF16)
    pvt_ref[...] = pv_ref[...].T.astype(BF16)


def _expert_tables_call(peer_u, peer_v):
    _, ne, d = peer_u.shape
    rows = EB_DENSE
    blk = pl.BlockSpec((None, rows, d), lambda i: (0, i, 0))
    return pl.pallas_call(
        _expert_tables_kernel,
        grid=(ne // rows,),
        in_specs=[blk, blk],
        out_specs=[pl.BlockSpec((rows, d), lambda i: (i, 0)), pl.BlockSpec((None, d, rows), lambda i: (i, 0, 0))],
        out_shape=[jax.ShapeDtypeStruct((ne, d), BF16), jax.ShapeDtypeStruct((ne // rows, d, rows), BF16)],
        compiler_params=pltpu.CompilerParams(dimension_semantics=("arbitrary",), vmem_limit_bytes=VMEM_LIMIT),
        name="expert_tables",
    )(peer_u, peer_v)


def _top16(s, key_iota):
    rank = jnp.full(s.shape, float(PEER_TOPK), F32)
    vals = []
    for kth in range(PEER_TOPK):
        m = jnp.max(s, axis=0, keepdims=True)
        idx = jnp.min(jnp.where(s == m, key_iota, float(PEER_N_KEYS)), axis=0, keepdims=True)
        hit = key_iota == idx
        rank = jnp.where(hit, float(kth), rank)
        s = jnp.where(hit, -jnp.inf, s)
        vals.append(m)
    return rank, vals


def _route_exact(s1, s2):
    lanes = s1.shape[1]
    key_iota = lax.broadcasted_iota(jnp.int32, (PEER_N_KEYS, lanes), 0).astype(F32)
    cand_iota = lax.broadcasted_iota(jnp.int32, (PEER_TOPK * PEER_TOPK, lanes), 0).astype(F32)
    k16 = lax.broadcasted_iota(jnp.int32, (PEER_TOPK, lanes), 0)
    rank1, v1 = _top16(s1, key_iota)
    rank2, v2 = _top16(s2, key_iota)
    top2 = jnp.zeros((PEER_TOPK, lanes), F32)
    for b in range(PEER_TOPK):
        top2 = jnp.where(k16 == b, v2[b], top2)
    cand = jnp.concatenate([v1[a] + top2 for a in range(PEER_TOPK)], axis=0)
    sel = jnp.zeros(cand.shape, F32)
    cmax = v1[0] + v2[0]
    z = jnp.zeros((1, lanes), F32)
    for _ in range(PEER_TOPK):
        m = jnp.max(cand, axis=0, keepdims=True)
        idx = jnp.min(jnp.where(cand == m, cand_iota, float(PEER_TOPK * PEER_TOPK)), axis=0, keepdims=True)
        hit = cand_iota == idx
        sel = jnp.where(hit, 1.0, sel)
        cand = jnp.where(hit, -jnp.inf, cand)
        z = z + jnp.exp(m - cmax)
    cnt = jnp.zeros((PEER_N_KEYS, lanes), F32)
    for a in range(PEER_TOPK):
        n_a = jnp.sum(sel[a * PEER_TOPK:(a + 1) * PEER_TOPK], axis=0, keepdims=True)
        cnt = jnp.where(rank1 == float(a), n_a, cnt)
    return cnt, rank2, jnp.exp(s1 - v1[0]), jnp.exp(s2 - v2[0]) * (GELU_HALF / z)


def _hi_lo(a, b):
    if a is None or b is None:
        return (b if a is None else a), None
    return jnp.maximum(a, b), jnp.minimum(a, b)


def _sort16_desc(v):
    n = len(v)
    k = 2
    while k <= n:
        j = k // 2
        while j >= 1:
            for i in range(n):
                l = i ^ j
                if l > i:
                    hi, lo = _hi_lo(v[i], v[l])
                    v[i], v[l] = (hi, lo) if (i & k) == 0 else (lo, hi)
            j //= 2
        k *= 2


def _merge16_desc(v):
    j = len(v) // 2
    while j >= 1:
        for i in range(len(v)):
            l = i ^ j
            if l > i:
                v[i], v[l] = _hi_lo(v[i], v[l])
        j //= 2


def _sublane_allsum(x):
    for shift in (4, 2, 1):
        x = x + pltpu.roll(x, shift, 0)
    return x


def _sorted_top16(tiles):
    v = list(tiles)
    _sort16_desc(v)
    for shift in (4, 2, 1):
        partner = [None if x is None else pltpu.roll(x, shift, 0) for x in reversed(v)]
        v = [_hi_lo(a, b)[0] for a, b in zip(v, partner)]
        _merge16_desc(v)
    return v


def _route_fast(s1, s2):
    nt = PEER_N_KEYS // SUBLANES
    lanes = s1.shape[1]
    sub = lax.broadcasted_iota(jnp.int32, (SUBLANES, lanes), 0)
    ninf = jnp.full((SUBLANES, lanes), -jnp.inf, F32)
    zero = jnp.zeros((SUBLANES, lanes), F32)
    t1 = [s1[r * SUBLANES:(r + 1) * SUBLANES, :] for r in range(nt)]
    t2 = [s2[r * SUBLANES:(r + 1) * SUBLANES, :] for r in range(nt)]
    v1 = _sorted_top16(t1)
    v2 = _sorted_top16(t2)

    rank2 = []
    for x in t2:
        r = jnp.full((SUBLANES, lanes), float(PEER_TOPK), F32)
        for k in range(PEER_TOPK - 1, -1, -1):
            r = jnp.where(x >= v2[k], float(k), r)
        rank2.append(r)

    def rows(v, base):
        out = v[base]
        for b in range(1, SUBLANES):
            out = jnp.where(sub == b, v[base + b], out)
        return out

    r1 = [rows(v1, 0), rows(v1, SUBLANES)]
    r2 = [rows(v2, 0), rows(v2, SUBLANES)]
    cands = [v1[0] + r2[0], v1[0] + r2[1], v1[1] + r2[0],
             jnp.where(sub < 5, v1[2] + r2[0], ninf), jnp.where(sub < 4, v1[3] + r2[0], ninf),
             jnp.where(sub >= 4, v2[0] + r1[0], ninf), v2[0] + r1[1],
             jnp.where(sub >= 4, v2[1] + r1[0], ninf), jnp.where(sub == 4, v2[2] + r1[0], ninf)]
    top = _sorted_top16(cands + [None] * (PEER_TOPK - len(cands)))
    tau, cmax = top[PEER_TOPK - 1], top[0]
    z = zero
    for c in cands:
        z = z + jnp.where(c >= tau, jnp.exp(c - cmax), 0.0)
    z = _sublane_allsum(z)

    cnt = []
    for x in t1:
        c = zero
        for b in range(3):
            c = jnp.where(x + v2[b] >= tau, float(b + 1), c)
        cnt.append(c)
    for a in range(4):
        n_a = zero
        for b in range(PEER_TOPK // (a + 1)):
            n_a = jnp.where(v1[a] + v2[b] >= tau, float(b + 1), n_a)
        cnt = [jnp.where(x == v1[a], n_a, c) for x, c in zip(t1, cnt)]

    total = _sublane_allsum(functools.reduce(lambda a, b: a + b, cnt))
    ranked = _sublane_allsum(functools.reduce(lambda a, b: a + b,
                                              [jnp.where(r < float(PEER_TOPK), 1.0, 0.0) for r in rank2]))
    bad = jnp.where(total != float(PEER_TOPK), 1.0, 0.0) + jnp.where(ranked != float(PEER_TOPK), 1.0, 0.0)
    for k in range(PEER_TOPK - 1):
        bad = bad + jnp.where(v2[k] == v2[k + 1], 1.0, 0.0)
    for k in range(4):
        bad = bad + jnp.where(v1[k] == v1[k + 1], 1.0, 0.0)

    rz = GELU_HALF / z
    cat = lambda tiles: jnp.concatenate(tiles, axis=0)
    return (cat(cnt), cat(rank2), cat([jnp.exp(x - v1[0]) for x in t1]),
            cat([jnp.exp(x - v2[0]) * rz for x in t2]), bad)


def _peer_route_kernel(xn2_ref, wqt_ref, sub_ref, cnt_ref, rnk_ref, c1_ref, e2_ref, sc_s):
    tp = xn2_ref.shape[0]
    lanes = LANES

    qt = _dot_nt(wqt_ref[...], xn2_ref[...]).astype(BF16)
    for hp in range(2 * PEER_HEADS):
        sc_s[hp] = _dot(sub_ref[hp % 2], qt[hp * PEER_HALF:(hp + 1) * PEER_HALF, :])

    def head_body(h, carry):
        for lg in range(tp // lanes):
            cols = slice(lg * lanes, (lg + 1) * lanes)
            s1 = sc_s[h * 2, :, cols]
            s2 = sc_s[h * 2 + 1, :, cols]

            def store(cnt, rank2, c1, e2):
                cnt_ref[h, :, cols] = cnt
                rnk_ref[h, :, cols] = rank2.astype(BF16)
                c1_ref[h, :, cols] = c1
                e2_ref[h, :, cols] = e2.astype(BF16)

            *tables, bad = _route_fast(s1, s2)
            store(*tables)

            @pl.when(jnp.max(bad) > 0.0)
            def _():
                store(*_route_exact(s1, s2))
        return carry

    lax.fori_loop(0, PEER_HEADS, head_body, 0)


def _peer_route_call(xn2, wqt_b, sub_b):
    n = xn2.shape[0]
    tp = TP_PRE
    tab = pl.BlockSpec((PEER_HEADS, PEER_N_KEYS, tp), lambda i: (0, 0, i))
    return pl.pallas_call(
        _peer_route_kernel,
        grid=(n // tp,),
        in_specs=[pl.BlockSpec((tp, D_MODEL), lambda i: (i, 0)),
                  _const_spec((PEER_HEADS * 2 * PEER_HALF, D_MODEL)), _const_spec((2, PEER_N_KEYS, PEER_HALF))],
        out_specs=[tab] * 4,
        out_shape=[jax.ShapeDtypeStruct((PEER_HEADS, PEER_N_KEYS, n), dt) for dt in (F32, BF16, F32, BF16)],
        scratch_shapes=[pltpu.VMEM((2 * PEER_HEADS, PEER_N_KEYS, tp), F32)],
        compiler_params=pltpu.CompilerParams(dimension_semantics=("arbitrary",), vmem_limit_bytes=VMEM_LIMIT),
        name="peer_route",
    )(xn2, wqt_b, sub_b)


def _peer_dense_kernel(xn2_ref, pu_ref, pvt_ref, *refs):
    ncg = xn2_ref.shape[0] // LANES
    cnt_refs, c1_refs = refs[:ncg], refs[ncg:2 * ncg]
    rnk_ref, e2_ref, h_ref, gf_ref, y_ref, acc_ref, at_s, g_s, rnk_s, e2_s, xt_s = refs[2 * ncg:]
    k = pl.program_id(1)
    kps = 4
    eb = kps * PEER_N_KEYS
    tm = xn2_ref.shape[0]
    sub_rows = BF16_TILE_ROWS
    ntile = PEER_N_KEYS // sub_rows

    @pl.when(k == 0)
    def _():
        acc_ref[...] = jnp.zeros_like(acc_ref)
        rnk_s[...] = rnk_ref[...]
        e2_s[...] = e2_ref[...]
        xt_s[...] = xn2_ref[...].astype(F32).T.astype(BF16)

    def weights_and_gate(row0, base):
        for cg in range(tm // LANES):
            cols = slice(cg * LANES, (cg + 1) * LANES)
            w = [[None] * ntile for _ in range(kps)]
            for h in range(PEER_HEADS):
                cnt_b = [cnt_refs[cg][h, pl.ds(row0 + ii, sub_rows, stride=0), :].astype(BF16) for ii in range(kps)]
                c1_b = [c1_refs[cg][h, pl.ds(row0 + ii, sub_rows, stride=0), :].astype(BF16) for ii in range(kps)]
                for r in range(ntile):
                    jr = slice(r * sub_rows, (r + 1) * sub_rows)
                    rt = rnk_s[h, jr, cols]
                    et = e2_s[h, jr, cols]
                    for ii in range(kps):
                        term = jnp.where(rt < cnt_b[ii], et, 0.0) * c1_b[ii]
                        w[ii][r] = term if w[ii][r] is None else w[ii][r] + term
            for ii in range(kps):
                for r in range(ntile):
                    e0 = base + ii * PEER_N_KEYS + r * sub_rows
                    er = slice(e0, e0 + sub_rows)
                    a = at_s[er, cols]
                    g_s[er, cols] = (a * (1.0 + lax.erf(a * (2.0 ** -0.5)))).astype(BF16) * w[ii][r]

    at_s[...] = _dot(pu_ref[...], xt_s[...])
    for b in range(at_s.shape[0] // eb):
        weights_and_gate(b * kps, b * eb)
    acc_ref[...] += _dot(pvt_ref[...], g_s[...])

    @pl.when(k == pl.num_programs(1) - 1)
    def _():
        y_ref[...] = _rms(h_ref[...] + acc_ref[...].T, gf_ref[...])


def _peer_dense_call(xn2, pu_b, pvt_b, cnt, rnk, c1, e2, h, gf):
    n = xn2.shape[0]
    tm, eb = min(TM_DENSE, n), EB_DENSE
    nstep = PEER_N_EXPERTS // eb
    tok = lambda dt: pl.BlockSpec((tm, D_MODEL), lambda t, k: (t, 0))
    tab = pl.BlockSpec((PEER_HEADS, PEER_N_KEYS, tm), lambda t, k: (0, 0, t))
    ncg = tm // LANES
    rows = [pl.BlockSpec((PEER_HEADS, eb // PEER_N_KEYS, LANES), lambda t, k, cg=cg: (0, k, t * ncg + cg))
            for cg in range(ncg)]
    return pl.pallas_call(
        _peer_dense_kernel,
        grid=(n // tm, nstep),
        in_specs=[tok(BF16), pl.BlockSpec((eb, D_MODEL), lambda t, k: (k, 0)),
                  pl.BlockSpec((None, D_MODEL, eb), lambda t, k: (k, 0, 0)),
                  *rows, *rows, tab, tab, tok(F32), pl.BlockSpec((1, D_MODEL), lambda t, k: (0, 0))],
        out_specs=tok(F32),
        out_shape=jax.ShapeDtypeStruct((n, D_MODEL), F32),
        scratch_shapes=[pltpu.VMEM((D_MODEL, tm), F32), pltpu.VMEM((eb, tm), F32), pltpu.VMEM((eb, tm), BF16),
                        pltpu.VMEM((PEER_HEADS, PEER_N_KEYS, tm), BF16),
                        pltpu.VMEM((PEER_HEADS, PEER_N_KEYS, tm), BF16),
                        pltpu.VMEM((D_MODEL, tm), BF16)],
        compiler_params=pltpu.CompilerParams(dimension_semantics=("arbitrary", "arbitrary"),
                                             vmem_limit_bytes=VMEM_LIMIT),
        name="peer_dense",
    )(xn2, pu_b, pvt_b, *([cnt] * ncg), *([c1] * ncg), rnk, e2, h, gf)


def _peer_and_final_norm(h, xn2, wqt_b, sub_b, pu_b, pvt_b, gf):
    cnt, rnk, c1, e2 = _peer_route_call(xn2, wqt_b, sub_b)
    return _peer_dense_call(xn2, pu_b, pvt_b, cnt, rnk, c1, e2, h, gf)


def kernel(x_prompt, x_sample, cache_swa_k, cache_swa_v, cache_mem_k, cache_mem_v, mem_prompt,
           norm1_g, w_in, b_gate, swa_sinks, chunk_norm_g, chunk_ws, chunk_bs, mem_norm_g, w_mem_kv,
           w_branch_out, w_out, norm2_g, peer_wq, peer_subkeys, peer_u, peer_v, final_norm_g):
    depth = norm1_g.shape[0]
    assert depth == 1, "single-layer step"
    batch, seq, _ = x_prompt.shape
    nseq, dec_seq, _ = x_sample.shape

    g1 = norm1_g[0][None]
    g2 = norm2_g[0][None]
    gf = final_norm_g[None]
    cng = chunk_norm_g[0][None]
    w_in_b = w_in[0].astype(BF16)
    wbo_b = w_branch_out[0].astype(BF16)
    wout_b = w_out[0].astype(BF16)
    wqt_b = peer_wq[0].T.astype(BF16)
    sub_b = peer_subkeys[0].astype(BF16)
    pu_b, pvt_b = _expert_tables_call(peer_u, peer_v)
    sinks = swa_sinks[0]
    bst = chunk_bs[0].T
    ws4 = chunk_ws[0][:, :dec_seq, :dec_seq].reshape(-1)
    bs4 = chunk_bs[0][:, :dec_seq].reshape(-1)

    mk, mv, mkb, mvb = _memkv_call(mem_prompt, mem_norm_g[0][None], w_mem_kv[0].astype(BF16))
    hp, xn2p, kp, vp = _prompt_layer_call(
        x_prompt.reshape(batch * seq, D_MODEL), mkb, mvb, sinks, g1, w_in_b, cng, chunk_ws[0], bst,
        b_gate[0], wbo_b, wout_b, g2, batch=batch, seq=seq)
    y_prompt = _peer_and_final_norm(hp, xn2p, wqt_b, sub_b, pu_b, pvt_b, gf).reshape(batch, seq, D_MODEL)

    xs = x_sample.reshape(nseq * dec_seq, D_MODEL)
    qs, qp, kn, vn, vcs, achk = _sample_proj_call(xs, ws4, bs4, g1, w_in_b, cng, dec_seq=dec_seq)
    aswa, amem, nk, nv = _sample_attn_call(
        sinks, qs, qp, kn, vn,
        cache_swa_k.reshape(nseq, WINDOW, KV_WIDTH), cache_swa_v.reshape(nseq, WINDOW, KV_WIDTH),
        cache_mem_k.reshape(nseq, MEM_LEN * MEM_HEADS, MEM_HEAD_DIM),
        cache_mem_v.reshape(nseq, MEM_LEN * MEM_HEADS, MEM_HEAD_DIM),
        dec_seq=dec_seq)
    hs, xn2s = _sample_merge_call(xs, aswa, achk, amem, g1, w_in_b, b_gate[0], wbo_b, wout_b, g2)
    y_sample = _peer_and_final_norm(hs, xn2s, wqt_b, sub_b, pu_b, pvt_b, gf).reshape(nseq, dec_seq, D_MODEL)

    kv5 = lambda a, b: a.reshape(1, b, WINDOW, SWA_KV_HEADS, SWA_HEAD_DIM)
    mem5 = lambda a: a.reshape(1, batch, MEM_LEN, MEM_HEADS, MEM_HEAD_DIM)
    return (y_prompt, y_sample, kv5(kp, batch), kv5(vp, batch), mem5(mk), mem5(mv),
            kv5(nk, nseq), kv5(nv, nseq), vcs.reshape(1, nseq, dec_seq, CHUNK_GROUPS, CHUNK_GROUP_DIM))
```
